```python
import jax, jax.numpy as jnp
from jax import lax
import numpy as np

D_MODEL = 1024
BATCH = 16
SEQ = 4096
DEPTH = 4

GRID_W = 64
CTX_LEN = 256
HEAD_DIM = 64
N_Q_HEADS = (D_MODEL // 2) // HEAD_DIM
N_KV_HEADS = max(N_Q_HEADS // 4, 1)
GQA_GROUP = N_Q_HEADS // N_KV_HEADS
ATTN_W = N_Q_HEADS * HEAD_DIM
KV_W = N_KV_HEADS * HEAD_DIM
CONF_W = D_MODEL // 4
CONF_CONV = 31
SCONV_W = D_MODEL - ATTN_W - CONF_W
SHORT_CONV = 3
MIX_W = ATTN_W + CONF_W + SCONV_W
IN_W = ATTN_W + 2 * KV_W + 2 * CONF_W + 3 * SCONV_W
Q_BLOCK = 128
ROPE_THETA = 10000.0
ROPE_AXIS_DIM = HEAD_DIM // 2
N_EXPERTS = 32
TOP_K = 4
D_EXPERT = D_MODEL
SWIGLU_ALPHA = 1.702
SWIGLU_LIMIT = 7.0
EXPERT_BLOCK = 256
N_ADA = 6
EPS = 1e-6

kernel_name = 'hybrid_parallel_heads_dit_moe'


def rms_norm(x, g):
    xf = x.astype(jnp.float32)
    y = xf * lax.rsqrt(jnp.mean(xf * xf, axis=-1, keepdims=True) + EPS)
    return (y * g.astype(jnp.float32)).astype(x.dtype)


def layer_norm(x, g, b):
    xf = x.astype(jnp.float32)
    mu = jnp.mean(xf, axis=-1, keepdims=True)
    var = jnp.mean(jnp.square(xf - mu), axis=-1, keepdims=True)
    y = (xf - mu) * lax.rsqrt(var + EPS)
    return (y * g.astype(jnp.float32) + b.astype(jnp.float32)).astype(x.dtype)


def modulate(x, g, shift, scale):
    return rms_norm(x, g) * (1 + scale) + shift


def depthwise_conv(x, w):
    pad = (w.shape[0] - 1) // 2
    return lax.conv_general_dilated(x, w[:, None, :].astype(x.dtype), (1,), [(pad, pad)],
                                    dimension_numbers=('NWC', 'WIO', 'NWC'),
                                    feature_group_count=x.shape[-1])


def axial_rope_tables(rows):
    row = jnp.repeat(jnp.arange(rows, dtype=jnp.float32), GRID_W)
    col = jnp.tile(jnp.arange(GRID_W, dtype=jnp.float32), rows)
    inv = ROPE_THETA ** (-jnp.arange(0, ROPE_AXIS_DIM, 2, dtype=jnp.float32) / ROPE_AXIS_DIM)
    ang = jnp.concatenate([row[:, None] * inv, col[:, None] * inv], axis=-1)
    return jnp.cos(ang), jnp.sin(ang)


def apply_axial_rope(x, cos, sin):
    xf = x.astype(jnp.float32)
    half = ROPE_AXIS_DIM // 2
    parts = []
    for a in range(2):
        xa = xf[..., a * ROPE_AXIS_DIM:(a + 1) * ROPE_AXIS_DIM]
        x1, x2 = xa[..., :half], xa[..., half:]
        ca = cos[:, None, a * half:(a + 1) * half]
        sa = sin[:, None, a * half:(a + 1) * half]
        parts += [x1 * ca - x2 * sa, x2 * ca + x1 * sa]
    return jnp.concatenate(parts, axis=-1).astype(x.dtype)


def heads(t, n):
    return t.reshape(t.shape[:-1] + (n, HEAD_DIM))


def split_in(p):
    return jnp.split(p, [ATTN_W, ATTN_W + KV_W, ATTN_W + 2 * KV_W, ATTN_W + 2 * KV_W + 2 * CONF_W], axis=-1)


def gqa_softmax(q, k, v):
    s = jnp.einsum('bqhgd,bkhd->bhgqk', q, k, preferred_element_type=jnp.float32) * (HEAD_DIM ** -0.5)
    p = jax.nn.softmax(s, axis=-1).astype(v.dtype)
    return jnp.einsum('bhgqk,bkhd->bqhgd', p, v)


def latent_attention(q, k, v, k_ctx, v_ctx):
    b, s = q.shape[:2]
    keys = jnp.concatenate([k, k_ctx], axis=1)
    vals = jnp.concatenate([v, v_ctx], axis=1)
    nb = s // Q_BLOCK
    qb = q.reshape(b, nb, Q_BLOCK, N_KV_HEADS, GQA_GROUP, HEAD_DIM).swapaxes(0, 1)
    out = lax.map(lambda qblk: gqa_softmax(qblk, keys, vals), qb)
    return out.swapaxes(0, 1).reshape(b, s, ATTN_W)


def conv_mixers(conf_in, sc_in, w_dw31, b_dw31, g_ln, b_ln, w_dw3):
    a, gate = jnp.split(conf_in, 2, axis=-1)
    u = depthwise_conv(a * jax.nn.sigmoid(gate), w_dw31) + b_dw31
    u = jax.nn.silu(layer_norm(u, g_ln, b_ln))
    bg, cg, hv = jnp.split(sc_in, 3, axis=-1)
    s = bg * depthwise_conv(cg * hv, w_dw3)
    return u, s


def merge_groups(attn, conf, sc, g_grp, w_out):
    y = jnp.concatenate([rms_norm(attn, g_grp[:ATTN_W]),
                         rms_norm(conf, g_grp[ATTN_W:ATTN_W + CONF_W]),
                         rms_norm(sc, g_grp[ATTN_W + CONF_W:])], axis=-1)
    return y @ w_out


def moe_ffn(h, w_r, b_r, w1, b1, w2, b2):
    n_tok = h.shape[0]
    n_pairs = n_tok * TOP_K
    logits = jnp.matmul(h, w_r, preferred_element_type=jnp.float32) + b_r.astype(jnp.float32)
    top_v, top_i = lax.top_k(logits, TOP_K)
    probs = jax.nn.softmax(top_v, axis=-1)
    e_flat = top_i.reshape(-1)
    tok_flat = jnp.arange(n_pairs, dtype=jnp.int32) // TOP_K
    w_flat = probs.reshape(-1)
    order = jnp.argsort(e_flat)
    e_sorted = e_flat[order]
    counts = jnp.bincount(e_flat, length=N_EXPERTS)
    starts = jnp.cumsum(counts) - counts
    padded = (counts + EXPERT_BLOCK - 1) // EXPERT_BLOCK * EXPERT_BLOCK
    pad_ends = jnp.cumsum(padded)
    dest = (pad_ends - padded)[e_sorted] + jnp.arange(n_pairs, dtype=jnp.int32) - starts[e_sorted]
    n_blocks = -(-n_pairs // EXPERT_BLOCK) + N_EXPERTS
    cap = n_blocks * EXPERT_BLOCK
    tok_buf = jnp.zeros((cap,), jnp.int32).at[dest].set(tok_flat[order])
    wt_buf = jnp.zeros((cap,), jnp.float32).at[dest].set(w_flat[order])
    blk_expert = jnp.minimum(jnp.searchsorted(pad_ends, jnp.arange(n_blocks) * EXPERT_BLOCK, side='right'),
                             N_EXPERTS - 1)

    def expert_block(args):
        tok, wt, e = args
        u = h[tok] @ w1[e] + b1[e]
        gate = jnp.minimum(u[:, 0::2], SWIGLU_LIMIT)
        up = jnp.clip(u[:, 1::2], -SWIGLU_LIMIT, SWIGLU_LIMIT)
        glu = gate * jax.nn.sigmoid(SWIGLU_ALPHA * gate)
        y = ((up + 1) * glu) @ w2[e] + b2[e]
        return y.astype(jnp.float32) * wt[:, None]

    ys = lax.map(expert_block, (tok_buf.reshape(n_blocks, EXPERT_BLOCK),
                                wt_buf.reshape(n_blocks, EXPERT_BLOCK), blk_expert))
    out = jnp.zeros((n_tok, h.shape[1]), jnp.float32).at[tok_buf].add(ys.reshape(cap, h.shape[1]))
    return out.astype(h.dtype)


def setup_inputs(seed: int = 0) -> dict:
    key = jax.random.key(seed)
    ks = jax.random.split(key, 24)
    f32 = jnp.float32
    nrm = jax.random.normal

    def w(k, shape, fan_in, scale=1.0):
        return nrm(k, shape, f32) * (scale * fan_in ** -0.5)

    def gain(k, shape):
        return 1.0 + 0.02 * nrm(k, shape, f32)

    def bias(k, shape, s=0.02):
        return s * nrm(k, shape, f32)

    return {
        'x': nrm(ks[0], (BATCH, SEQ, D_MODEL), f32),
        'c': nrm(ks[1], (BATCH, D_MODEL), f32),
        'ctx': nrm(ks[2], (BATCH, CTX_LEN, D_MODEL), f32),
        'c_ctx': nrm(ks[3], (D_MODEL,), f32),
        'w_ada': w(ks[4], (DEPTH, D_MODEL, N_ADA * D_MODEL), D_MODEL, 0.5),
        'b_ada': bias(ks[5], (DEPTH, N_ADA * D_MODEL)),
        'g_norm_mix': gain(ks[6], (DEPTH, D_MODEL)),
        'g_norm_ffn': gain(ks[7], (DEPTH, D_MODEL)),
        'w_in': w(ks[8], (DEPTH, D_MODEL, IN_W), D_MODEL),
        'g_q': gain(ks[9], (DEPTH, HEAD_DIM)),
        'g_k': gain(ks[10], (DEPTH, HEAD_DIM)),
        'w_conf_dw': w(ks[11], (DEPTH, CONF_CONV, CONF_W), CONF_CONV),
        'b_conf_dw': bias(ks[12], (DEPTH, CONF_W)),
        'g_conf_ln': gain(ks[13], (DEPTH, CONF_W)),
        'b_conf_ln': bias(ks[14], (DEPTH, CONF_W)),
        'w_sc_dw': w(ks[15], (DEPTH, SHORT_CONV, SCONV_W), SHORT_CONV),
        'g_grp': gain(ks[16], (DEPTH, MIX_W)),
        'w_out': w(ks[17], (DEPTH, MIX_W, D_MODEL), MIX_W),
        'w_router': w(ks[18], (DEPTH, D_MODEL, N_EXPERTS), D_MODEL),
        'b_router': bias(ks[19], (DEPTH, N_EXPERTS), 0.01),
        'w_e_in': w(ks[20], (DEPTH, N_EXPERTS, D_MODEL, 2 * D_EXPERT), D_MODEL),
        'b_e_in': bias(ks[21], (DEPTH, N_EXPERTS, 2 * D_EXPERT)),
        'w_e_out': w(ks[22], (DEPTH, N_EXPERTS, D_EXPERT, D_MODEL), D_EXPERT),
        'b_e_out': bias(ks[23], (DEPTH, N_EXPERTS, D_MODEL)),
    }


def reference(x, c, ctx, c_ctx, w_ada, b_ada, g_norm_mix, g_norm_ffn, w_in, g_q, g_k,
              w_conf_dw, b_conf_dw, g_conf_ln, b_conf_ln, w_sc_dw, g_grp, w_out,
              w_router, b_router, w_e_in, b_e_in, w_e_out, b_e_out):
    b, s, d = x.shape
    n_ctx = ctx.shape[1]
    rows = s // GRID_W
    cos, sin = axial_rope_tables(rows)
    silu_c = jax.nn.silu(c)
    silu_cc = jax.nn.silu(c_ctx)
    xc = ctx
    for l in range(DEPTH):
        last = l == DEPTH - 1
        mod = silu_c @ w_ada[l] + b_ada[l]
        mod_c = silu_cc @ w_ada[l] + b_ada[l]
        sh1, sc1, gt1, sh2, sc2, gt2 = jnp.split(mod[:, None, :], N_ADA, axis=-1)
        csh1, csc1, cgt1, csh2, csc2, cgt2 = jnp.split(mod_c, N_ADA, axis=-1)

        h = modulate(x, g_norm_mix[l], sh1, sc1)
        hc = modulate(xc, g_norm_mix[l], csh1, csc1)
        q, k, v, conf_in, sc_in = split_in(h @ w_in[l])
        q = apply_axial_rope(rms_norm(heads(q, N_Q_HEADS), g_q[l]), cos, sin)
        k = apply_axial_rope(rms_norm(heads(k, N_KV_HEADS), g_k[l]), cos, sin)
        v = heads(v, N_KV_HEADS)
        if last:
            k_c, v_c = jnp.split(hc @ w_in[l][:, ATTN_W:ATTN_W + 2 * KV_W], 2, axis=-1)
        else:
            q_c, k_c, v_c, conf_c, sc_c = split_in(hc @ w_in[l])
        k_c = rms_norm(heads(k_c, N_KV_HEADS), g_k[l])
        v_c = heads(v_c, N_KV_HEADS)

        attn = latent_attention(q, k, v, k_c, v_c)
        conf, sc = conv_mixers(conf_in, sc_in, w_conf_dw[l], b_conf_dw[l], g_conf_ln[l], b_conf_ln[l], w_sc_dw[l])
        x = x + gt1 * merge_groups(attn, conf, sc, g_grp[l], w_out[l])

        if not last:
            q_c = rms_norm(heads(q_c, N_Q_HEADS), g_q[l])
            attn_c = gqa_softmax(q_c.reshape(b, n_ctx, N_KV_HEADS, GQA_GROUP, HEAD_DIM), k_c, v_c)
            attn_c = attn_c.reshape(b, n_ctx, ATTN_W)
            conf_cx, sc_cx = conv_mixers(conf_c, sc_c, w_conf_dw[l], b_conf_dw[l], g_conf_ln[l], b_conf_ln[l], w_sc_dw[l])
            xc = xc + cgt1 * merge_groups(attn_c, conf_cx, sc_cx, g_grp[l], w_out[l])

        h2 = modulate(x, g_norm_ffn[l], sh2, sc2).reshape(b * s, d)
        if last:
            f = moe_ffn(h2, w_router[l], b_router[l], w_e_in[l], b_e_in[l], w_e_out[l], b_e_out[l])
            x = x + gt2 * f.reshape(b, s, d)
        else:
            h2c = modulate(xc, g_norm_ffn[l], csh2, csc2).reshape(b * n_ctx, d)
            f = moe_ffn(jnp.concatenate([h2, h2c], axis=0), w_router[l], b_router[l],
                        w_e_in[l], b_e_in[l], w_e_out[l], b_e_out[l])
            x = x + gt2 * f[:b * s].reshape(b, s, d)
            xc = xc + cgt2 * f[b * s:].reshape(b, n_ctx, d)
    return x
```

```python
import functools

import jax
import jax.numpy as jnp
from jax import lax
from jax.experimental import pallas as pl
from jax.experimental.pallas import tpu as pltpu

GRID_W = 64
HEAD_DIM = 64
ROPE_THETA = 10000.0
TOP_K = 4
SWIGLU_ALPHA = 1.702
SWIGLU_LIMIT = 7.0
EPS = 1e-6
N_ADA = 6
LANES = 128
NEG_BIG = -1e30

TOKEN_TILE = 512
Q_TILE = 256
KEY_CHUNK = 512
CONV_TILE = 256
CONF_HALO = 16
SC_HALO = 8
EXPERT_TILE = 512
VMEM_LIMIT = 48 * 1024 * 1024

bf16 = jnp.bfloat16
f32 = jnp.float32


def _dot(a, b):
    return jnp.dot(a, b, preferred_element_type=f32)


def _sigmoid(x):
    return 1.0 / (1.0 + jnp.exp(-x))


def _split_bf16(x):
    hi = x.astype(bf16)
    lo = (x - hi.astype(f32)).astype(bf16)
    return hi, lo


def _ada_kernel(c_ref, w_ref, b_ref, o_ref):
    c = c_ref[...]
    s = c * _sigmoid(c)
    s_hi, s_lo = _split_bf16(s)
    w_hi, w_lo = _split_bf16(w_ref[0])
    o_ref[0] = _dot(s_hi, w_hi) + _dot(s_lo, w_hi) + _dot(s_hi, w_lo) + b_ref[0]


def _ada_call(cc, w_ada, b_ada):
    depth, d, n6 = w_ada.shape
    rows = cc.shape[0]
    tn = 1536
    return pl.pallas_call(
        _ada_kernel,
        grid=(depth, n6 // tn),
        in_specs=[pl.BlockSpec((rows, d), lambda l, j: (0, 0)),
                  pl.BlockSpec((1, d, tn), lambda l, j: (l, 0, j)),
                  pl.BlockSpec((1, 1, tn), lambda l, j: (l, 0, j))],
        out_specs=pl.BlockSpec((1, rows, tn), lambda l, j: (l, 0, j)),
        out_shape=jax.ShapeDtypeStruct((depth, rows, n6), f32),
        compiler_params=pltpu.CompilerParams(vmem_limit_bytes=VMEM_LIMIT),
    )(cc, w_ada, b_ada.reshape(depth, 1, n6))


def _modulated(x, g, mod, shift_row, scale_row):
    ms = jnp.mean(x * x, axis=-1, keepdims=True)
    xh = x * lax.rsqrt(ms + EPS)
    return xh * (g * (1.0 + mod[scale_row:scale_row + 1, :])) + mod[shift_row:shift_row + 1, :]


def _head_sumsq(y):
    n = y.shape[1]
    r = lax.broadcasted_iota(jnp.int32, (n, n), 0) // HEAD_DIM
    c = lax.broadcasted_iota(jnp.int32, (n, n), 1) // HEAD_DIM
    ones_bd = jnp.where(r == c, 1.0, 0.0).astype(bf16)
    return _dot((y * y).astype(bf16), ones_bd)


def _rope(y, cos, sin_signed):
    lane = lax.broadcasted_iota(jnp.int32, y.shape, 1)
    partner = jnp.where((lane % 32) < 16, pltpu.roll(y, LANES - 16, 1), pltpu.roll(y, 16, 1))
    return y * cos + partner * sin_signed


def _inproj_kernel(x_ref, mod_ref, g_ref, w_ref, gq_ref, gk_ref, cos_ref, sin_ref,
                   q_ref, k_ref, v_ref, conf_ref, sc_ref, *, attn_w, kv_w, conf_w2):
    h = _modulated(x_ref[...], g_ref[...], mod_ref[0], 0, 1).astype(bf16)
    cos = cos_ref[...]
    sin = sin_ref[...]
    o_k = attn_w
    o_v = o_k + kv_w
    o_conf = o_v + kv_w
    o_sc = o_conf + conf_w2

    q = _dot(h, w_ref[:, 0:attn_w])
    for cb in range(attn_w // 256):
        qb = q[:, cb * 256:(cb + 1) * 256]
        qn = qb * lax.rsqrt(_head_sumsq(qb) * (1.0 / HEAD_DIM) + EPS) * gq_ref[:, cb * 256:(cb + 1) * 256]
        for half in range(2):
            lo = cb * 256 + half * LANES
            y = _rope(qn[:, half * LANES:(half + 1) * LANES], cos, sin) * (HEAD_DIM ** -0.5)
            q_ref[:, lo:lo + LANES] = y.astype(bf16)

    k = _dot(h, w_ref[:, o_k:o_v])
    kn = k * lax.rsqrt(_head_sumsq(k) * (1.0 / HEAD_DIM) + EPS) * gk_ref[...]
    k_ref[...] = _rope(kn, cos, sin).astype(bf16)

    v = _dot(h, w_ref[:, o_v:o_conf])
    lane = lax.broadcasted_iota(jnp.int32, v.shape, 1)
    v_ref[:, 0:LANES] = jnp.where(lane < HEAD_DIM, v, 1.0).astype(bf16)
    v_ref[:, LANES:2 * LANES] = jnp.where(lane < HEAD_DIM, pltpu.roll(v, HEAD_DIM, 1), 1.0).astype(bf16)

    conf_ref[...] = _dot(h, w_ref[:, o_conf:o_sc])
    sc_ref[...] = _dot(h, w_ref[:, o_sc:])


def _seg_index(i, tile, seg_len, n_seg):
    return jnp.minimum(i * tile // seg_len, n_seg)


def _inproj_call(x, mods, g, w, gq, gk, cos_t, sin_t, *, b, s, attn_w, kv_w, conf_w2, sc_w3):
    n, d = x.shape
    tm = TOKEN_TILE
    n_lat_tiles = b * s // tm
    tiles_per_seq = s // tm
    in_w = w.shape[1]

    def tab_map(i):
        return (jnp.where(i < n_lat_tiles, i % tiles_per_seq, tiles_per_seq), 0)

    kern = functools.partial(_inproj_kernel, attn_w=attn_w, kv_w=kv_w, conf_w2=conf_w2)
    return pl.pallas_call(
        kern,
        grid=(n // tm,),
        in_specs=[pl.BlockSpec((tm, d), lambda i: (i, 0)),
                  pl.BlockSpec((1, N_ADA, d), lambda i: (_seg_index(i, tm, s, b), 0, 0)),
                  pl.BlockSpec((1, d), lambda i: (0, 0)),
                  pl.BlockSpec((d, in_w), lambda i: (0, 0)),
                  pl.BlockSpec((1, attn_w), lambda i: (0, 0)),
                  pl.BlockSpec((1, kv_w), lambda i: (0, 0)),
                  pl.BlockSpec((tm, LANES), tab_map),
                  pl.BlockSpec((tm, LANES), tab_map)],
        out_specs=[pl.BlockSpec((tm, attn_w), lambda i: (i, 0)),
                   pl.BlockSpec((tm, kv_w), lambda i: (i, 0)),
                   pl.BlockSpec((tm, 2 * kv_w), lambda i: (i, 0)),
                   pl.BlockSpec((tm, conf_w2), lambda i: (i, 0)),
                   pl.BlockSpec((tm, sc_w3), lambda i: (i, 0))],
        out_shape=[jax.ShapeDtypeStruct((n, attn_w), bf16),
                   jax.ShapeDtypeStruct((n, kv_w), bf16),
                   jax.ShapeDtypeStruct((n, 2 * kv_w), bf16),
                   jax.ShapeDtypeStruct((n, conf_w2), f32),
                   jax.ShapeDtypeStruct((n, sc_w3), f32)],
        compiler_params=pltpu.CompilerParams(vmem_limit_bytes=VMEM_LIMIT),
    )(x, mods, g, w, gq, gk, cos_t, sin_t)


def _attn_kernel(q_ref, k1_ref, v1_ref, k2_ref, v2_ref, g_ref, o_ref, qs_ref, m_ref, acc_ref,
                 *, n_chunks1, chunk1, n_pairs):
    tq = q_ref.shape[0]
    rows = n_pairs * tq
    lane = lax.broadcasted_iota(jnp.int32, (tq, LANES), 1)
    outs = []
    for grp in range(2):
        in_group = (lane < HEAD_DIM) if grp == 0 else (lane >= HEAD_DIM)
        for j in range(n_pairs):
            qj = q_ref[:, j * LANES:(j + 1) * LANES]
            qs_ref[j * tq:(j + 1) * tq, :] = jnp.where(in_group, qj, jnp.zeros_like(qj))
        m_ref[...] = jnp.full(m_ref.shape, NEG_BIG, f32)
        acc_ref[...] = jnp.zeros(acc_ref.shape, f32)

        def step(kblk, vblk):
            s = lax.dot_general(qs_ref[...], kblk, (((1,), (1,)), ((), ())), preferred_element_type=f32)
            m_old = m_ref[...]
            m_new = jnp.maximum(m_old, jnp.max(s, axis=-1, keepdims=True))
            alpha = jnp.exp(m_old - m_new)
            p = jnp.exp(s - m_new[:, 0:1]).astype(bf16)
            acc_ref[...] = acc_ref[...] * jnp.concatenate([alpha, alpha], axis=1) + _dot(p, vblk)
            m_ref[...] = m_new

        if n_chunks1 > 0:
            def body(c, carry):
                start = pl.multiple_of(c * chunk1, chunk1)
                step(k1_ref[pl.ds(start, chunk1), :], v1_ref[pl.ds(start, chunk1), :])
                return carry
            lax.fori_loop(0, n_chunks1, body, 0)
        step(k2_ref[...], v2_ref[...])

        blk = acc_ref[:, grp * LANES:(grp + 1) * LANES]
        rolled = pltpu.roll(blk, HEAD_DIM, 1)
        outs.append(blk / rolled if grp == 0 else rolled / blk)

    ss = jnp.zeros((tq, 1), f32)
    pair_out = []
    for j in range(n_pairs):
        o = jnp.where(lane < HEAD_DIM, outs[0][j * tq:(j + 1) * tq, :], outs[1][j * tq:(j + 1) * tq, :])
        pair_out.append(o)
        ss = ss + jnp.sum(o * o, axis=-1, keepdims=True)
    inv = lax.rsqrt(ss * (1.0 / (n_pairs * LANES)) + EPS)
    for j in range(n_pairs):
        o_ref[:, j * LANES:(j + 1) * LANES] = (pair_out[j] * inv * g_ref[:, j * LANES:(j + 1) * LANES]).astype(bf16)


def _attn_call(q, k, v, g, *, b, s, ctx, latent):
    n, attn_w = q.shape
    n_pairs = attn_w // LANES
    kv_w = k.shape[1]
    ctx_blk0 = b * s // ctx
    if latent:
        tq = Q_TILE
        per_b = s // tq
        grid = (b, per_b)
        q_map = lambda bi, j: (bi * per_b + j, 0)
        kv1_rows, n_chunks1, chunk1 = s, s // KEY_CHUNK, KEY_CHUNK
        kv1_map = lambda bi, j: (bi, 0)
        kv2_map = lambda bi, j: (ctx_blk0 + bi, 0)
    else:
        tq = ctx
        grid = (b, 1)
        q_map = lambda bi, j: (ctx_blk0 + bi, 0)
        kv1_rows, n_chunks1, chunk1 = ctx, 0, ctx
        kv1_map = lambda bi, j: (ctx_blk0 + bi, 0)
        kv2_map = kv1_map
    kern = functools.partial(_attn_kernel, n_chunks1=n_chunks1, chunk1=chunk1, n_pairs=n_pairs)
    return pl.pallas_call(
        kern,
        grid=grid,
        in_specs=[pl.BlockSpec((tq, attn_w), q_map),
                  pl.BlockSpec((kv1_rows, kv_w), kv1_map),
                  pl.BlockSpec((kv1_rows, 2 * kv_w), kv1_map),
                  pl.BlockSpec((ctx, kv_w), kv2_map),
                  pl.BlockSpec((ctx, 2 * kv_w), kv2_map),
                  pl.BlockSpec((1, attn_w), lambda bi, j: (0, 0))],
        out_specs=pl.BlockSpec((tq, attn_w), q_map),
        out_shape=jax.ShapeDtypeStruct((n, attn_w), bf16),
        scratch_shapes=[pltpu.VMEM((n_pairs * tq, LANES), bf16),
                        pltpu.VMEM((n_pairs * tq, LANES), f32),
                        pltpu.VMEM((n_pairs * tq, 2 * LANES), f32)],
        compiler_params=pltpu.CompilerParams(vmem_limit_bytes=VMEM_LIMIT),
    )(q, k, v, k, v, g)


def _conv_kernel(cp_ref, cc_ref, cn_ref, sp_ref, scc_ref, sn_ref, w31_ref, b31_ref, gln_ref, bln_ref,
                 w3_ref, gc_ref, gs_ref, o_ref, ext_ref, ext3_ref,
                 *, n_lat_tiles, tiles_per_seq, cw, taps31, taps3):
    i = pl.program_id(0)
    tc = cc_ref.shape[0]
    is_lat = i < n_lat_tiles
    left_edge = jnp.where(is_lat, (i % tiles_per_seq) == 0, True)
    right_edge = jnp.where(is_lat, (i % tiles_per_seq) == tiles_per_seq - 1, True)

    def glu(blk):
        return blk[:, 0:cw] * _sigmoid(blk[:, cw:2 * cw])

    ext_ref[0:CONF_HALO, :] = jnp.where(left_edge, 0.0, glu(cp_ref[...]))
    ext_ref[CONF_HALO:CONF_HALO + tc, :] = glu(cc_ref[...])
    ext_ref[CONF_HALO + tc:, :] = jnp.where(right_edge, 0.0, glu(cn_ref[...]))
    pad31 = (taps31 - 1) // 2
    u = jnp.zeros((tc, cw), f32) + b31_ref[...]
    for t in range(taps31):
        off = CONF_HALO - pad31 + t
        u = u + ext_ref[off:off + tc, :] * w31_ref[t:t + 1, :]
    mu = jnp.mean(u, axis=-1, keepdims=True)
    var = jnp.mean(jnp.square(u - mu), axis=-1, keepdims=True)
    y = (u - mu) * lax.rsqrt(var + EPS) * gln_ref[...] + bln_ref[...]
    conf = y * _sigmoid(y)

    def ch(blk):
        return blk[:, cw:2 * cw] * blk[:, 2 * cw:3 * cw]

    ext3_ref[0:SC_HALO, :] = jnp.where(left_edge, 0.0, ch(sp_ref[...]))
    ext3_ref[SC_HALO:SC_HALO + tc, :] = ch(scc_ref[...])
    ext3_ref[SC_HALO + tc:, :] = jnp.where(right_edge, 0.0, ch(sn_ref[...]))
    pad3 = (taps3 - 1) // 2
    acc = jnp.zeros((tc, cw), f32)
    for t in range(taps3):
        off = SC_HALO - pad3 + t
        acc = acc + ext3_ref[off:off + tc, :] * w3_ref[t:t + 1, :]
    sc = scc_ref[:, 0:cw] * acc

    def grp_norm(z, g_ref_):
        return z * lax.rsqrt(jnp.mean(z * z, axis=-1, keepdims=True) + EPS) * g_ref_[...]

    o_ref[:, 0:cw] = grp_norm(conf, gc_ref).astype(bf16)
    o_ref[:, cw:2 * cw] = grp_norm(sc, gs_ref).astype(bf16)


def _conv_call(conf_in, sc_in, w31, b31, gln, bln, w3, gc, gs, *, b, s, ctx):
    n = conf_in.shape[0]
    cw = w31.shape[1]
    tc = CONV_TILE
    assert ctx == tc and s % tc == 0
    n_tiles = n // tc
    ch_per = tc // CONF_HALO
    sh_per = tc // SC_HALO
    kern = functools.partial(_conv_kernel, n_lat_tiles=b * s // tc, tiles_per_seq=s // tc, cw=cw,
                             taps31=w31.shape[0], taps3=w3.shape[0])
    small = lambda a: pl.BlockSpec(a.shape, lambda i: (0, 0))
    return pl.pallas_call(
        kern,
        grid=(n_tiles,),
        in_specs=[pl.BlockSpec((CONF_HALO, 2 * cw), lambda i: (jnp.maximum(i * ch_per - 1, 0), 0)),
                  pl.BlockSpec((tc, 2 * cw), lambda i: (i, 0)),
                  pl.BlockSpec((CONF_HALO, 2 * cw), lambda i: (jnp.minimum((i + 1) * ch_per, n_tiles * ch_per - 1), 0)),
                  pl.BlockSpec((SC_HALO, 3 * cw), lambda i: (jnp.maximum(i * sh_per - 1, 0), 0)),
                  pl.BlockSpec((tc, 3 * cw), lambda i: (i, 0)),
                  pl.BlockSpec((SC_HALO, 3 * cw), lambda i: (jnp.minimum((i + 1) * sh_per, n_tiles * sh_per - 1), 0)),
                  small(w31), small(b31), small(gln), small(bln), small(w3), small(gc), small(gs)],
        out_specs=pl.BlockSpec((tc, 2 * cw), lambda i: (i, 0)),
        out_shape=jax.ShapeDtypeStruct((n, 2 * cw), bf16),
        scratch_shapes=[pltpu.VMEM((tc + 2 * CONF_HALO, cw), f32),
                        pltpu.VMEM((tc + 2 * SC_HALO, cw), f32)],
        compiler_params=pltpu.CompilerParams(vmem_limit_bytes=VMEM_LIMIT),
    )(conf_in, conf_in, conf_in, sc_in, sc_in, sc_in, w31, b31, gln, bln, w3, gc, gs)


def _outproj_kernel(x_ref, ya_ref, yb_ref, w_ref, mod_ref, g_ref, wr_ref, br_ref,
                    xo_ref, h_ref, idx_ref, prob_ref, *, attn_w):
    mod = mod_ref[0]
    y = _dot(ya_ref[...], w_ref[0:attn_w, :]) + _dot(yb_ref[...], w_ref[attn_w:, :])
    xn = x_ref[...] + mod[2:3, :] * y
    xo_ref[...] = xn
    h = _modulated(xn, g_ref[...], mod, 3, 4).astype(bf16)
    h_ref[...] = h

    logits = _dot(h, wr_ref[...]) + br_ref[...]
    lane = lax.broadcasted_iota(jnp.int32, logits.shape, 1).astype(f32)
    vals, idxs = [], []
    for _ in range(TOP_K):
        m = jnp.max(logits, axis=-1, keepdims=True)
        first = jnp.min(jnp.where(logits == m, lane, float(LANES)), axis=-1, keepdims=True)
        vals.append(m)
        idxs.append(first)
        logits = jnp.where(lane == first, NEG_BIG, logits)
    es = [jnp.exp(vv - vals[0]) for vv in vals]
    den = es[0] + es[1] + es[2] + es[3]
    idx_out = jnp.zeros(logits.shape, f32)
    prob_out = jnp.zeros(logits.shape, f32)
    for r in range(TOP_K):
        idx_out = jnp.where(lane == float(r), idxs[r], idx_out)
        prob_out = jnp.where(lane == float(r), es[r] / den, prob_out)
    idx_ref[...] = idx_out.astype(jnp.int32)
    prob_ref[...] = prob_out


def _outproj_call(x, ya, yb, w, mods, g, wr, br, *, b, s):
    n, d = x.shape
    tm = TOKEN_TILE
    attn_w = ya.shape[1]
    kern = functools.partial(_outproj_kernel, attn_w=attn_w)
    row = lambda i: (i, 0)
    fixed = lambda i: (0, 0)
    return pl.pallas_call(
        kern,
        grid=(n // tm,),
        in_specs=[pl.BlockSpec((tm, d), row),
                  pl.BlockSpec((tm, attn_w), row),
                  pl.BlockSpec((tm, yb.shape[1]), row),
                  pl.BlockSpec(w.shape, fixed),
                  pl.BlockSpec((1, N_ADA, d), lambda i: (_seg_index(i, tm, s, b), 0, 0)),
                  pl.BlockSpec((1, d), fixed),
                  pl.BlockSpec(wr.shape, fixed),
                  pl.BlockSpec((1, LANES), fixed)],
        out_specs=[pl.BlockSpec((tm, d), row), pl.BlockSpec((tm, d), row),
                   pl.BlockSpec((tm, LANES), row), pl.BlockSpec((tm, LANES), row)],
        out_shape=[jax.ShapeDtypeStruct((n, d), f32), jax.ShapeDtypeStruct((n, d), bf16),
                   jax.ShapeDtypeStruct((n, LANES), jnp.int32), jax.ShapeDtypeStruct((n, LANES), f32)],
        input_output_aliases={0: 0},
        compiler_params=pltpu.CompilerParams(vmem_limit_bytes=VMEM_LIMIT),
    )(x, ya, yb, w, mods, g, wr, br)


def _moe_kernel(be_ref, nu_ref, xs_ref, w1g_ref, w1u_ref, b1g_ref, b1u_ref, w2_ref, b2_ref, ys_ref):
    @pl.when(pl.program_id(0) < nu_ref[0])
    def _():
        xs = xs_ref[...]
        gate = jnp.minimum(_dot(xs, w1g_ref[0]) + b1g_ref[0], SWIGLU_LIMIT)
        up = jnp.clip(_dot(xs, w1u_ref[0]) + b1u_ref[0], -SWIGLU_LIMIT, SWIGLU_LIMIT)
        glu = gate * _sigmoid(SWIGLU_ALPHA * gate)
        ys_ref[...] = _dot(((up + 1.0) * glu).astype(bf16), w2_ref[0]) + b2_ref[0]


def _moe_call(blk_expert, n_used, xs, w1g, w1u, b1g, b1u, w2, b2):
    cap, d = xs.shape
    bm = EXPERT_TILE
    de = w1g.shape[2]
    row = lambda i, be, nu: (jnp.minimum(i, nu[0] - 1), 0)
    exp3 = lambda i, be, nu: (be[i], 0, 0)
    return pl.pallas_call(
        _moe_kernel,
        grid_spec=pltpu.PrefetchScalarGridSpec(
            num_scalar_prefetch=2,
            grid=(cap // bm,),
            in_specs=[pl.BlockSpec((bm, d), row),
                      pl.BlockSpec((1, d, de), exp3),
                      pl.BlockSpec((1, d, de), exp3),
                      pl.BlockSpec((1, 1, de), exp3),
                      pl.BlockSpec((1, 1, de), exp3),
                      pl.BlockSpec((1, de, d), exp3),
                      pl.BlockSpec((1, 1, d), exp3)],
            out_specs=pl.BlockSpec((bm, d), row)),
        out_shape=jax.ShapeDtypeStruct((cap, d), f32),
        compiler_params=pltpu.CompilerParams(vmem_limit_bytes=VMEM_LIMIT),
    )(blk_expert, n_used, xs, w1g, w1u, b1g, b1u, w2, b2)


def _combine_kernel(x_ref, f_ref, mod_ref, o_ref):
    o_ref[...] = x_ref[...] + mod_ref[0][5:6, :] * f_ref[...]


def _combine_call(x, f, mods, *, b, s):
    n, d = x.shape
    tm = TOKEN_TILE
    row = lambda i: (i, 0)
    return pl.pallas_call(
        _combine_kernel,
        grid=(n // tm,),
        in_specs=[pl.BlockSpec((tm, d), row), pl.BlockSpec((tm, d), row),
                  pl.BlockSpec((1, N_ADA, d), lambda i: (_seg_index(i, tm, s, b), 0, 0))],
        out_specs=pl.BlockSpec((tm, d), row),
        out_shape=jax.ShapeDtypeStruct((n, d), f32),
        input_output_aliases={0: 0},
        compiler_params=pltpu.CompilerParams(vmem_limit_bytes=VMEM_LIMIT),
    )(x, f, mods)


def _dispatch_plan(top_i, n_experts):
    n_tok = top_i.shape[0]
    n_pairs = n_tok * TOP_K
    bm = EXPERT_TILE
    e_flat = top_i.reshape(-1)
    order = jnp.argsort(e_flat, stable=True).astype(jnp.int32)
    e_sorted = e_flat[order]
    counts = jnp.zeros((n_experts,), jnp.int32).at[e_flat].add(1)
    starts = jnp.cumsum(counts) - counts
    padded = (counts + bm - 1) // bm * bm
    pad_ends = jnp.cumsum(padded)
    dest_sorted = (pad_ends - padded)[e_sorted] + jnp.arange(n_pairs, dtype=jnp.int32) - starts[e_sorted]
    n_blocks = -(-n_pairs // bm) + n_experts
    cap = n_blocks * bm
    tok_buf = jnp.zeros((cap,), jnp.int32).at[dest_sorted].set(order // TOP_K)
    pos = jnp.zeros((n_pairs,), jnp.int32).at[order].set(dest_sorted).reshape(n_tok, TOP_K)
    blk_expert = jnp.minimum(jnp.searchsorted(pad_ends, jnp.arange(n_blocks, dtype=jnp.int32) * bm, side='right'),
                             n_experts - 1).astype(jnp.int32)
    n_used = (pad_ends[-1] // bm).astype(jnp.int32).reshape(1)
    return tok_buf, pos, blk_expert, n_used


def _rope_tables(s, extra_rows):
    rows = s // GRID_W
    axis_dim = HEAD_DIM // 2
    half = axis_dim // 2
    row = jnp.repeat(jnp.arange(rows, dtype=f32), GRID_W)
    col = jnp.tile(jnp.arange(GRID_W, dtype=f32), rows)
    inv = ROPE_THETA ** (-jnp.arange(0, axis_dim, 2, dtype=f32) / axis_dim)
    ang = jnp.concatenate([row[:, None] * inv, col[:, None] * inv], axis=-1)
    lane = jnp.arange(LANES)
    src = ((lane % HEAD_DIM) // axis_dim) * half + lane % half
    sign = jnp.where((lane % axis_dim) < half, -1.0, 1.0).astype(f32)
    cos_t = jnp.cos(ang)[:, src]
    sin_t = jnp.sin(ang)[:, src] * sign
    cos_t = jnp.concatenate([cos_t, jnp.ones((extra_rows, LANES), f32)], axis=0)
    sin_t = jnp.concatenate([sin_t, jnp.zeros((extra_rows, LANES), f32)], axis=0)
    return cos_t, sin_t


def kernel(x, c, ctx, c_ctx, w_ada, b_ada, g_norm_mix, g_norm_ffn, w_in, g_q, g_k, w_conf_dw, b_conf_dw,
           g_conf_ln, b_conf_ln, w_sc_dw, g_grp, w_out, w_router, b_router, w_e_in, b_e_in, w_e_out, b_e_out):
    b, s, d = x.shape
    n_ctx = ctx.shape[1]
    depth = w_ada.shape[0]
    n_experts = w_router.shape[2]
    attn_w = d // 2
    n_q = attn_w // HEAD_DIM
    n_kv = max(n_q // 4, 1)
    kv_w = n_kv * HEAD_DIM
    conf_w = d // 4
    sconv_w = d - attn_w - conf_w
    assert n_kv == 2 and kv_w == LANES and conf_w == sconv_w
    n_lat = b * s
    n_tok = n_lat + b * n_ctx
    assert s % TOKEN_TILE == 0 and (b * n_ctx) % TOKEN_TILE == 0 and s % KEY_CHUNK == 0 and s % Q_TILE == 0

    perm = jnp.concatenate([jnp.concatenate([jnp.arange(HEAD_DIM) + j * HEAD_DIM,
                                             jnp.arange(HEAD_DIM) + (j + n_q // 2) * HEAD_DIM])
                            for j in range(n_q // 2)])
    col_perm = jnp.concatenate([perm, jnp.arange(attn_w, w_in.shape[2])])
    w_in_b = w_in[:, :, col_perm].astype(bf16)
    w_out_b = w_out[:, col_perm[:d], :].astype(bf16)
    g_grp_p = g_grp[:, col_perm[:d]]
    gq_t = jnp.tile(g_q, (1, n_q)).reshape(depth, 1, attn_w)
    gk_t = jnp.tile(g_k, (1, n_kv)).reshape(depth, 1, kv_w)
    cos_t, sin_t = _rope_tables(s, TOKEN_TILE)

    w1g = w_e_in[..., 0::2].astype(bf16)
    w1u = w_e_in[..., 1::2].astype(bf16)
    b1g = b_e_in[..., 0::2].reshape(depth, n_experts, 1, -1)
    b1u = b_e_in[..., 1::2].reshape(depth, n_experts, 1, -1)
    w2 = w_e_out.astype(bf16)
    b2 = b_e_out.reshape(depth, n_experts, 1, d)
    wr = jnp.pad(w_router, ((0, 0), (0, 0), (0, LANES - n_experts))).astype(bf16)
    br = jnp.pad(b_router, ((0, 0), (0, LANES - n_experts)), constant_values=NEG_BIG).reshape(depth, 1, LANES)

    mod_rows = b + 1
    pad_rows = -mod_rows % 8
    cc = jnp.concatenate([c, c_ctx[None, :], jnp.zeros((pad_rows, d), f32)], axis=0)
    mods = _ada_call(cc, w_ada, b_ada)[:, :mod_rows].reshape(depth, mod_rows, N_ADA, d)

    xf = jnp.concatenate([x.reshape(n_lat, d), ctx.reshape(b * n_ctx, d)], axis=0)
    for l in range(depth):
        q, k, v, conf_in, sc_in = _inproj_call(
            xf, mods[l], g_norm_mix[l][None], w_in_b[l], gq_t[l], gk_t[l], cos_t, sin_t,
            b=b, s=s, attn_w=attn_w, kv_w=kv_w, conf_w2=2 * conf_w, sc_w3=3 * sconv_w)
        g_attn = g_grp_p[l][None, :attn_w]
        ya = _attn_call(q, k, v, g_attn, b=b, s=s, ctx=n_ctx, latent=True)
        ya_c = _attn_call(q, k, v, g_attn, b=b, s=s, ctx=n_ctx, latent=False)
        ya = jnp.concatenate([ya[:n_lat], ya_c[n_lat:]], axis=0)
        yb = _conv_call(conf_in, sc_in, w_conf_dw[l], b_conf_dw[l][None], g_conf_ln[l][None], b_conf_ln[l][None],
                        w_sc_dw[l], g_grp[l][None, attn_w:attn_w + conf_w], g_grp[l][None, attn_w + conf_w:],
                        b=b, s=s, ctx=n_ctx)
        xf, h2, idx, prob = _outproj_call(xf, ya, yb, w_out_b[l], mods[l], g_norm_ffn[l][None], wr[l], br[l],
                                          b=b, s=s)
        top_i = idx[:, :TOP_K]
        top_p = prob[:, :TOP_K]
        tok_buf, pos, blk_expert, n_used = _dispatch_plan(top_i, n_experts)
        ys = _moe_call(blk_expert, n_used, h2[tok_buf], w1g[l], w1u[l], b1g[l], b1u[l], w2[l], b2[l])
        f = jnp.sum(ys[pos] * top_p[:, :, None], axis=1)
        xf = _combine_call(xf, f, mods[l], b=b, s=s)
    return xf[:n_lat].reshape(b, s, d)
```

```python
import functools
import math

import jax
import jax.numpy as jnp
from jax import lax
from jax.experimental import pallas as pl
from jax.experimental.pallas import tpu as pltpu

GRID_W = 64
HEAD_DIM = 64
ROPE_THETA = 10000.0
TOP_K = 4
SWIGLU_ALPHA = 1.702
SWIGLU_LIMIT = 7.0
EPS = 1e-6
N_ADA = 6
LANES = 128
MXU_W = 256
NEG_BIG = -1e30
LOG2E = math.log2(math.e)

TOKEN_TILE = 512
Q_TILE = 256
KEY_CHUNK = 512
CONV_TILE = 256
CONF_HALO = 16
SC_HALO = 8
EXPERT_TILE = 512
ONES_ROWS = 16
VMEM_LIMIT = 48 * 1024 * 1024
ROUTE_IDX, ROUTE_PROB, ROUTE_RANK = 0, TOP_K, 2 * TOP_K

bf16 = jnp.bfloat16
f32 = jnp.float32


def _dot(a, b):
    return jnp.dot(a, b, preferred_element_type=f32)


def _sigmoid(x):
    return 1.0 / (1.0 + jnp.exp(-x))


def _split_bf16(x):
    hi = x.astype(bf16)
    lo = (x - hi.astype(f32)).astype(bf16)
    return hi, lo


def _params():
    return pltpu.CompilerParams(vmem_limit_bytes=VMEM_LIMIT)


def _ada_kernel(c_ref, w_ref, b_ref, o_ref):
    c = c_ref[...]
    s = c * _sigmoid(c)
    s_hi, s_lo = _split_bf16(s)
    w_hi, w_lo = _split_bf16(w_ref[0])
    o_ref[0] = _dot(s_hi, w_hi) + _dot(s_lo, w_hi) + _dot(s_hi, w_lo) + b_ref[0]


def _ada_call(cc, w_ada, b_ada):
    depth, d, n6 = w_ada.shape
    rows = cc.shape[0]
    tn = 1536
    return pl.pallas_call(
        _ada_kernel,
        grid=(depth, n6 // tn),
        in_specs=[pl.BlockSpec((rows, d), lambda l, j: (0, 0)),
                  pl.BlockSpec((1, d, tn), lambda l, j: (l, 0, j)),
                  pl.BlockSpec((1, 1, tn), lambda l, j: (l, 0, j))],
        out_specs=pl.BlockSpec((1, rows, tn), lambda l, j: (l, 0, j)),
        out_shape=jax.ShapeDtypeStruct((depth, rows, n6), f32),
        compiler_params=_params(),
    )(cc, w_ada, b_ada.reshape(depth, 1, n6))


def _w1_prep_kernel(w_ref, o_ref):
    r = lax.broadcasted_iota(jnp.int32, (MXU_W, MXU_W), 0)
    c = lax.broadcasted_iota(jnp.int32, (MXU_W, MXU_W), 1)
    src = jnp.where(c < LANES, 2 * c, 2 * (c - LANES) + 1)
    sel = jnp.where(r == src, 1.0, 0.0).astype(bf16)
    for grp in range(w_ref.shape[2] // MXU_W):
        cols = slice(grp * MXU_W, (grp + 1) * MXU_W)
        o_ref[0, :, cols] = _dot(w_ref[0, :, cols].astype(bf16), sel).astype(bf16)


def _cast_kernel(w_ref, o_ref):
    o_ref[...] = w_ref[...].astype(bf16)


def _weights_call(kern, w, rows):
    ne, d, n = w.shape
    return pl.pallas_call(
        kern,
        grid=(ne, d // rows),
        in_specs=[pl.BlockSpec((1, rows, n), lambda e, j: (e, j, 0))],
        out_specs=pl.BlockSpec((1, rows, n), lambda e, j: (e, j, 0)),
        out_shape=jax.ShapeDtypeStruct(w.shape, bf16),
        compiler_params=_params(),
    )(w)


def _modulated(x, g, mod, shift_row, scale_row):
    ms = jnp.mean(x * x, axis=-1, keepdims=True)
    xh = x * lax.rsqrt(ms + EPS)
    return xh * (g * (1.0 + mod[scale_row:scale_row + 1, :])) + mod[shift_row:shift_row + 1, :]


def _head_sumsq(y):
    n = y.shape[1]
    r = lax.broadcasted_iota(jnp.int32, (n, n), 0) // HEAD_DIM
    c = lax.broadcasted_iota(jnp.int32, (n, n), 1) // HEAD_DIM
    ones_bd = jnp.where(r == c, 1.0, 0.0).astype(bf16)
    return _dot((y * y).astype(bf16), ones_bd)


def _rope(y, cos, sin_signed):
    lane = lax.broadcasted_iota(jnp.int32, y.shape, 1)
    partner = jnp.where((lane % 32) < 16, pltpu.roll(y, LANES - 16, 1), pltpu.roll(y, 16, 1))
    return y * cos + partner * sin_signed


def _seg_index(i, tile, seg_len, n_seg):
    return jnp.minimum(i * tile // seg_len, n_seg)


def _inproj_kernel(x_ref, mod_ref, g_ref, w_ref, gq_ref, gk_ref, cos_ref, sin_ref,
                   qt_ref, k_ref, vt_ref, conf_ref, sc_ref, *, attn_w, kv_w, conf_w2):
    h = _modulated(x_ref[...], g_ref[...], mod_ref[0], 0, 1).astype(bf16)
    cos = cos_ref[...]
    sin = sin_ref[...]
    o_k = attn_w
    o_v = o_k + kv_w
    o_conf = o_v + kv_w
    o_sc = o_conf + conf_w2

    q = _dot(h, w_ref[:, 0:attn_w])
    for cb in range(attn_w // MXU_W):
        qb = q[:, cb * MXU_W:(cb + 1) * MXU_W]
        qn = qb * lax.rsqrt(_head_sumsq(qb) * (1.0 / HEAD_DIM) + EPS) * gq_ref[:, cb * MXU_W:(cb + 1) * MXU_W]
        for half in range(2):
            lo = cb * MXU_W + half * LANES
            y = _rope(qn[:, half * LANES:(half + 1) * LANES], cos, sin) * (LOG2E * HEAD_DIM ** -0.5)
            qt_ref[lo:lo + LANES, :] = y.T.astype(bf16)

    k = _dot(h, w_ref[:, o_k:o_v])
    kn = k * lax.rsqrt(_head_sumsq(k) * (1.0 / HEAD_DIM) + EPS) * gk_ref[...]
    k_ref[...] = _rope(kn, cos, sin).astype(bf16)

    vt = _dot(h, w_ref[:, o_v:o_conf]).T.astype(bf16)
    ones = jnp.ones((HEAD_DIM, vt.shape[1]), bf16)
    for hh in range(kv_w // HEAD_DIM):
        vt_ref[2 * hh * HEAD_DIM:(2 * hh + 1) * HEAD_DIM, :] = vt[hh * HEAD_DIM:(hh + 1) * HEAD_DIM, :]
        vt_ref[(2 * hh + 1) * HEAD_DIM:(2 * hh + 2) * HEAD_DIM, :] = ones

    conf_ref[...] = _dot(h, w_ref[:, o_conf:o_sc])
    sc_ref[...] = _dot(h, w_ref[:, o_sc:])


def _inproj_call(x, mods, g, w, gq, gk, cos_t, sin_t, *, b, s, attn_w, kv_w, conf_w2, sc_w3):
    n, d = x.shape
    tm = TOKEN_TILE
    n_lat_tiles = b * s // tm
    tiles_per_seq = s // tm
    in_w = w.shape[1]

    def tab_map(i):
        return (jnp.where(i < n_lat_tiles, i % tiles_per_seq, tiles_per_seq), 0)

    kern = functools.partial(_inproj_kernel, attn_w=attn_w, kv_w=kv_w, conf_w2=conf_w2)
    return pl.pallas_call(
        kern,
        grid=(n // tm,),
        in_specs=[pl.BlockSpec((tm, d), lambda i: (i, 0)),
                  pl.BlockSpec((1, N_ADA, d), lambda i: (_seg_index(i, tm, s, b), 0, 0)),
                  pl.BlockSpec((1, d), lambda i: (0, 0)),
                  pl.BlockSpec((d, in_w), lambda i: (0, 0)),
                  pl.BlockSpec((1, attn_w), lambda i: (0, 0)),
                  pl.BlockSpec((1, kv_w), lambda i: (0, 0)),
                  pl.BlockSpec((tm, LANES), tab_map),
                  pl.BlockSpec((tm, LANES), tab_map)],
        out_specs=[pl.BlockSpec((attn_w, tm), lambda i: (0, i)),
                   pl.BlockSpec((tm, kv_w), lambda i: (i, 0)),
                   pl.BlockSpec((2 * kv_w, tm), lambda i: (0, i)),
                   pl.BlockSpec((tm, conf_w2), lambda i: (i, 0)),
                   pl.BlockSpec((tm, sc_w3), lambda i: (i, 0))],
        out_shape=[jax.ShapeDtypeStruct((attn_w, n), bf16),
                   jax.ShapeDtypeStruct((n, kv_w), bf16),
                   jax.ShapeDtypeStruct((2 * kv_w, n), bf16),
                   jax.ShapeDtypeStruct((n, conf_w2), f32),
                   jax.ShapeDtypeStruct((n, sc_w3), f32)],
        compiler_params=_params(),
    )(x, mods, g, w, gq, gk, cos_t, sin_t)


def _attn_kernel(*refs, n_chunks1, chunk1, n_pairs, aliased):
    if aliased:
        refs = refs[1:]
    qt_ref, k1_ref, vt1_ref, k2_ref, vt2_ref, g_ref, o_ref, qs_ref, m_ref, acc_ref = refs
    tq = qt_ref.shape[1]
    v_rows = HEAD_DIM + ONES_ROWS
    sub = lax.broadcasted_iota(jnp.int32, (LANES, tq), 0)
    for grp in range(2):
        in_group = (sub < HEAD_DIM) if grp == 0 else (sub >= HEAD_DIM)
        for j in range(n_pairs):
            qj = qt_ref[j * LANES:(j + 1) * LANES, :]
            qs_ref[grp, :, j * tq:(j + 1) * tq] = jnp.where(in_group, qj, jnp.zeros_like(qj))
    m_ref[...] = jnp.full(m_ref.shape, NEG_BIG, f32)
    acc_ref[...] = jnp.zeros(acc_ref.shape, f32)

    def step(grp, kblk, vtblk):
        st = _dot(kblk, qs_ref[grp])
        m_old = m_ref[grp]
        m_new = jnp.maximum(m_old, jnp.max(st, axis=0, keepdims=True))
        alpha = jnp.exp2(m_old - m_new)
        pt = jnp.exp2(st - m_new[0:1, :]).astype(bf16)
        acc_ref[grp] = acc_ref[grp] * alpha[0:1, :] + _dot(vtblk, pt)
        m_ref[grp] = m_new

    def both(kblk, vt_of):
        for grp in range(2):
            step(grp, kblk, vt_of(grp))

    if n_chunks1 > 0:
        def body(c, carry):
            start = pl.multiple_of(c * chunk1, chunk1)
            both(k1_ref[pl.ds(start, chunk1), :],
                 lambda grp: vt1_ref[grp * LANES:grp * LANES + v_rows, pl.ds(start, chunk1)])
            return carry
        lax.fori_loop(0, n_chunks1, body, 0)
    both(k2_ref[...], lambda grp: vt2_ref[grp * LANES:grp * LANES + v_rows, :])

    outs = []
    for grp in range(2):
        acc = acc_ref[grp]
        outs.append(acc[0:HEAD_DIM, :] / acc[HEAD_DIM:HEAD_DIM + 1, :])
    ot = jnp.concatenate([outs[grp][:, j * tq:(j + 1) * tq] for j in range(n_pairs) for grp in range(2)], axis=0)
    inv = lax.rsqrt(jnp.mean(ot * ot, axis=0, keepdims=True) + EPS)
    o_ref[...] = ((ot * inv).T * g_ref[...]).astype(bf16)


def _attn_call(qt, k, vt, g, prev, *, b, s, ctx, latent):
    attn_w, n = qt.shape
    n_pairs = attn_w // LANES
    kv_w = k.shape[1]
    ctx_blk0 = b * s // ctx
    if latent:
        tq = Q_TILE
        per_b = s // tq
        grid = (b, per_b)
        q_idx = lambda bi, j: bi * per_b + j
        kv1_rows, n_chunks1, chunk1 = s, s // KEY_CHUNK, KEY_CHUNK
        kv1_idx = lambda bi, j: bi
    else:
        tq = ctx
        grid = (b, 1)
        q_idx = lambda bi, j: ctx_blk0 + bi
        kv1_rows, n_chunks1, chunk1 = ctx, 0, ctx
        kv1_idx = q_idx
    kv2_idx = lambda bi, j: ctx_blk0 + bi
    aliased = prev is not None
    kern = functools.partial(_attn_kernel, n_chunks1=n_chunks1, chunk1=chunk1, n_pairs=n_pairs, aliased=aliased)
    in_specs = [pl.BlockSpec((attn_w, tq), lambda bi, j: (0, q_idx(bi, j))),
                pl.BlockSpec((kv1_rows, kv_w), lambda bi, j: (kv1_idx(bi, j), 0)),
                pl.BlockSpec((2 * kv_w, kv1_rows), lambda bi, j: (0, kv1_idx(bi, j))),
                pl.BlockSpec((ctx, kv_w), lambda bi, j: (kv2_idx(bi, j), 0)),
                pl.BlockSpec((2 * kv_w, ctx), lambda bi, j: (0, kv2_idx(bi, j))),
                pl.BlockSpec((1, attn_w), lambda bi, j: (0, 0))]
    args = [qt, k, vt, k, vt, g]
    if aliased:
        in_specs = [pl.BlockSpec(memory_space=pl.ANY)] + in_specs
        args = [prev] + args
    return pl.pallas_call(
        kern,
        grid=grid,
        in_specs=in_specs,
        out_specs=pl.BlockSpec((tq, attn_w), lambda bi, j: (q_idx(bi, j), 0)),
        out_shape=jax.ShapeDtypeStruct((n, attn_w), bf16),
        scratch_shapes=[pltpu.VMEM((2, LANES, n_pairs * tq), bf16),
                        pltpu.VMEM((2, 8, n_pairs * tq), f32),
                        pltpu.VMEM((2, HEAD_DIM + ONES_ROWS, n_pairs * tq), f32)],
        input_output_aliases={0: 0} if aliased else {},
        compiler_params=_params(),
    )(*args)


def _conv_kernel(cp_ref, cc_ref, cn_ref, sp_ref, scc_ref, sn_ref, w31_ref, b31_ref, gln_ref, bln_ref,
                 w3_ref, gc_ref, gs_ref, o_ref, ext_ref, ext3_ref,
                 *, n_lat_tiles, tiles_per_seq, cw, taps31, taps3):
    i = pl.program_id(0)
    tc = cc_ref.shape[0]
    is_lat = i < n_lat_tiles
    left_edge = jnp.where(is_lat, (i % tiles_per_seq) == 0, True)
    right_edge = jnp.where(is_lat, (i % tiles_per_seq) == tiles_per_seq - 1, True)

    def glu(blk):
        return blk[:, 0:cw] * _sigmoid(blk[:, cw:2 * cw])

    ext_ref[0:CONF_HALO, :] = jnp.where(left_edge, 0.0, glu(cp_ref[...]))
    ext_ref[CONF_HALO:CONF_HALO + tc, :] = glu(cc_ref[...])
    ext_ref[CONF_HALO + tc:, :] = jnp.where(right_edge, 0.0, glu(cn_ref[...]))
    pad31 = (taps31 - 1) // 2
    u = jnp.zeros((tc, cw), f32) + b31_ref[...]
    for t in range(taps31):
        off = CONF_HALO - pad31 + t
        u = u + ext_ref[off:off + tc, :] * w31_ref[t:t + 1, :]
    mu = jnp.mean(u, axis=-1, keepdims=True)
    var = jnp.mean(jnp.square(u - mu), axis=-1, keepdims=True)
    y = (u - mu) * lax.rsqrt(var + EPS) * gln_ref[...] + bln_ref[...]
    conf = y * _sigmoid(y)

    def ch(blk):
        return blk[:, cw:2 * cw] * blk[:, 2 * cw:3 * cw]

    ext3_ref[0:SC_HALO, :] = jnp.where(left_edge, 0.0, ch(sp_ref[...]))
    ext3_ref[SC_HALO:SC_HALO + tc, :] = ch(scc_ref[...])
    ext3_ref[SC_HALO + tc:, :] = jnp.where(right_edge, 0.0, ch(sn_ref[...]))
    pad3 = (taps3 - 1) // 2
    acc = jnp.zeros((tc, cw), f32)
    for t in range(taps3):
        off = SC_HALO - pad3 + t
        acc = acc + ext3_ref[off:off + tc, :] * w3_ref[t:t + 1, :]
    sc = scc_ref[:, 0:cw] * acc

    def grp_norm(z, g_ref_):
        return z * lax.rsqrt(jnp.mean(z * z, axis=-1, keepdims=True) + EPS) * g_ref_[...]

    o_ref[:, 0:cw] = grp_norm(conf, gc_ref).astype(bf16)
    o_ref[:, cw:2 * cw] = grp_norm(sc, gs_ref).astype(bf16)


def _conv_call(conf_in, sc_in, w31, b31, gln, bln, w3, gc, gs, *, b, s, ctx):
    n = conf_in.shape[0]
    cw = w31.shape[1]
    tc = CONV_TILE
    assert ctx == tc and s % tc == 0
    n_tiles = n // tc
    ch_per = tc // CONF_HALO
    sh_per = tc // SC_HALO
    kern = functools.partial(_conv_kernel, n_lat_tiles=b * s // tc, tiles_per_seq=s // tc, cw=cw,
                             taps31=w31.shape[0], taps3=w3.shape[0])
    small = lambda a: pl.BlockSpec(a.shape, lambda i: (0, 0))
    return pl.pallas_call(
        kern,
        grid=(n_tiles,),
        in_specs=[pl.BlockSpec((CONF_HALO, 2 * cw), lambda i: (jnp.maximum(i * ch_per - 1, 0), 0)),
                  pl.BlockSpec((tc, 2 * cw), lambda i: (i, 0)),
                  pl.BlockSpec((CONF_HALO, 2 * cw), lambda i: (jnp.minimum((i + 1) * ch_per, n_tiles * ch_per - 1), 0)),
                  pl.BlockSpec((SC_HALO, 3 * cw), lambda i: (jnp.maximum(i * sh_per - 1, 0), 0)),
                  pl.BlockSpec((tc, 3 * cw), lambda i: (i, 0)),
                  pl.BlockSpec((SC_HALO, 3 * cw), lambda i: (jnp.minimum((i + 1) * sh_per, n_tiles * sh_per - 1), 0)),
                  small(w31), small(b31), small(gln), small(bln), small(w3), small(gc), small(gs)],
        out_specs=pl.BlockSpec((tc, 2 * cw), lambda i: (i, 0)),
        out_shape=jax.ShapeDtypeStruct((n, 2 * cw), bf16),
        scratch_shapes=[pltpu.VMEM((tc + 2 * CONF_HALO, cw), f32),
                        pltpu.VMEM((tc + 2 * SC_HALO, cw), f32)],
        compiler_params=_params(),
    )(conf_in, conf_in, conf_in, sc_in, sc_in, sc_in, w31, b31, gln, bln, w3, gc, gs)


def _outproj_kernel(x_ref, ya_ref, yb_ref, w_ref, mod_ref, g_ref, wr_ref, br_ref,
                    xo_ref, h_ref, route_ref, cnt_ref, *, attn_w):
    @pl.when(pl.program_id(0) == 0)
    def _():
        cnt_ref[...] = jnp.zeros(cnt_ref.shape, f32)

    mod = mod_ref[0]
    y = _dot(ya_ref[...], w_ref[0:attn_w, :]) + _dot(yb_ref[...], w_ref[attn_w:, :])
    xn = x_ref[...] + mod[2:3, :] * y
    xo_ref[...] = xn
    h = _modulated(xn, g_ref[...], mod, 3, 4).astype(bf16)
    h_ref[...] = h

    logits = _dot(h, wr_ref[...]) + br_ref[...]
    tm = logits.shape[0]
    lane = lax.broadcasted_iota(jnp.int32, logits.shape, 1).astype(f32)
    vals, idxs = [], []
    chosen = jnp.zeros(logits.shape, f32)
    for _ in range(TOP_K):
        m = jnp.max(logits, axis=-1, keepdims=True)
        first = jnp.min(jnp.where(logits == m, lane, float(LANES)), axis=-1, keepdims=True)
        vals.append(m)
        idxs.append(first)
        hit = lane == first
        chosen = jnp.where(hit, 1.0, chosen)
        logits = jnp.where(hit, NEG_BIG, logits)
    es = [jnp.exp(vv - vals[0]) for vv in vals]
    den = es[0] + es[1] + es[2] + es[3]

    r = lax.broadcasted_iota(jnp.int32, (tm, tm), 0)
    c = lax.broadcasted_iota(jnp.int32, (tm, tm), 1)
    before = _dot(jnp.where(r > c, 1.0, 0.0).astype(bf16), chosen.astype(bf16)) + cnt_ref[0:1, :]
    cnt_ref[...] = cnt_ref[...] + jnp.sum(chosen, axis=0, keepdims=True)

    route = jnp.zeros(logits.shape, f32)
    for k in range(TOP_K):
        rank = jnp.sum(jnp.where(lane == idxs[k], before, 0.0), axis=-1, keepdims=True)
        route = jnp.where(lane == float(ROUTE_IDX + k), idxs[k], route)
        route = jnp.where(lane == float(ROUTE_PROB + k), es[k] / den, route)
        route = jnp.where(lane == float(ROUTE_RANK + k), rank, route)
    route_ref[...] = route


def _outproj_call(x, ya, yb, w, mods, g, wr, br, *, b, s):
    n, d = x.shape
    tm = TOKEN_TILE
    attn_w = ya.shape[1]
    kern = functools.partial(_outproj_kernel, attn_w=attn_w)
    row = lambda i: (i, 0)
    fixed = lambda i: (0, 0)
    return pl.pallas_call(
        kern,
        grid=(n // tm,),
        in_specs=[pl.BlockSpec((tm, d), row),
                  pl.BlockSpec((tm, attn_w), row),
                  pl.BlockSpec((tm, yb.shape[1]), row),
                  pl.BlockSpec(w.shape, fixed),
                  pl.BlockSpec((1, N_ADA, d), lambda i: (_seg_index(i, tm, s, b), 0, 0)),
                  pl.BlockSpec((1, d), fixed),
                  pl.BlockSpec(wr.shape, fixed),
                  pl.BlockSpec((1, LANES), fixed)],
        out_specs=[pl.BlockSpec((tm, d), row), pl.BlockSpec((tm, d), row),
                   pl.BlockSpec((tm, LANES), row), pl.BlockSpec((8, LANES), fixed)],
        out_shape=[jax.ShapeDtypeStruct((n, d), f32), jax.ShapeDtypeStruct((n, d), bf16),
                   jax.ShapeDtypeStruct((n, LANES), f32), jax.ShapeDtypeStruct((8, LANES), f32)],
        input_output_aliases={0: 0},
        compiler_params=_params(),
    )(x, ya, yb, w, mods, g, wr, br)


def _moe_kernel(be_ref, nu_ref, xs_ref, w1_ref, b1_ref, w2_ref, b2_ref, ys_ref):
    @pl.when(pl.program_id(0) < nu_ref[0])
    def _():
        u = _dot(xs_ref[...], w1_ref[0]) + b1_ref[0]
        acts = []
        for grp in range(u.shape[1] // MXU_W):
            gate = jnp.minimum(u[:, grp * MXU_W:grp * MXU_W + LANES], SWIGLU_LIMIT)
            up = jnp.clip(u[:, grp * MXU_W + LANES:(grp + 1) * MXU_W], -SWIGLU_LIMIT, SWIGLU_LIMIT)
            acts.append(((up + 1.0) * (gate * _sigmoid(SWIGLU_ALPHA * gate))).astype(bf16))
        ys_ref[...] = (_dot(jnp.concatenate(acts, axis=1), w2_ref[0]) + b2_ref[0]).astype(bf16)


def _moe_call(blk_expert, n_used, xs, w1, b1, w2, b2):
    cap, d = xs.shape
    bm = EXPERT_TILE
    de2 = w1.shape[2]
    row = lambda i, be, nu: (jnp.minimum(i, nu[0] - 1), 0)
    exp3 = lambda i, be, nu: (be[i], 0, 0)
    return pl.pallas_call(
        _moe_kernel,
        grid_spec=pltpu.PrefetchScalarGridSpec(
            num_scalar_prefetch=2,
            grid=(cap // bm,),
            in_specs=[pl.BlockSpec((bm, d), row),
                      pl.BlockSpec((1, d, de2), exp3),
                      pl.BlockSpec((1, 1, de2), exp3),
                      pl.BlockSpec((1, de2 // 2, d), exp3),
                      pl.BlockSpec((1, 1, d), exp3)],
            out_specs=pl.BlockSpec((bm, d), row)),
        out_shape=jax.ShapeDtypeStruct((cap, d), bf16),
        compiler_params=_params(),
    )(blk_expert, n_used, xs, w1, b1, w2, b2)


def _combine_kernel(x_ref, y0_ref, y1_ref, y2_ref, y3_ref, route_ref, mod_ref, o_ref):
    route = route_ref[...]
    f = jnp.zeros(x_ref.shape, f32)
    for k, y_ref in enumerate((y0_ref, y1_ref, y2_ref, y3_ref)):
        f = f + route[:, ROUTE_PROB + k:ROUTE_PROB + k + 1] * y_ref[...].astype(f32)
    o_ref[...] = x_ref[...] + mod_ref[0][5:6, :] * f


def _combine_call(x, ys4, route, mods, *, b, s):
    n, d = x.shape
    tm = TOKEN_TILE
    row = lambda i: (i, 0)
    return pl.pallas_call(
        _combine_kernel,
        grid=(n // tm,),
        in_specs=[pl.BlockSpec((tm, d), row)] + [pl.BlockSpec((tm, d), row)] * TOP_K +
                 [pl.BlockSpec((tm, LANES), row),
                  pl.BlockSpec((1, N_ADA, d), lambda i: (_seg_index(i, tm, s, b), 0, 0))],
        out_specs=pl.BlockSpec((tm, d), row),
        out_shape=jax.ShapeDtypeStruct((n, d), f32),
        input_output_aliases={0: 0},
        compiler_params=_params(),
    )(x, *ys4, route, mods)


def _dispatch_plan(route, counts_f, n_experts):
    n_tok = route.shape[0]
    n_pairs = n_tok * TOP_K
    bm = EXPERT_TILE
    pair_bits = (n_pairs - 1).bit_length()
    assert pair_bits + (n_experts - 1).bit_length() < 31
    top_i = route[:, ROUTE_IDX:ROUTE_IDX + TOP_K].astype(jnp.int32)
    rank = route[:, ROUTE_RANK:ROUTE_RANK + TOP_K].astype(jnp.int32)
    counts = counts_f[0, :n_experts].astype(jnp.int32)
    starts = jnp.cumsum(counts) - counts
    padded = (counts + bm - 1) // bm * bm
    pad_ends = jnp.cumsum(padded)
    pad_starts = pad_ends - padded
    experts = jnp.arange(n_experts, dtype=jnp.int32)
    pos = jnp.sum(jnp.where(top_i[:, :, None] == experts, pad_starts, 0), axis=-1) + rank
    n_blocks = -(-n_pairs // bm) + n_experts
    blk_start = jnp.arange(n_blocks, dtype=jnp.int32) * bm
    blk_expert = jnp.minimum(jnp.sum(blk_start[:, None] >= pad_ends[None, :], axis=-1), n_experts - 1).astype(jnp.int32)
    n_used = (pad_ends[-1] // bm).astype(jnp.int32).reshape(1)
    pair_id = jnp.arange(n_pairs, dtype=jnp.int32)
    keys = jnp.sort(top_i.reshape(-1) * (1 << pair_bits) + pair_id)
    sorted_tok = (keys & ((1 << pair_bits) - 1)) // TOP_K
    shift = (starts - pad_starts)[blk_expert]
    src = jnp.clip((blk_start + shift)[:, None] + jnp.arange(bm, dtype=jnp.int32)[None, :], 0, n_pairs - 1)
    tok_buf = sorted_tok[src.reshape(-1)]
    return tok_buf, pos, blk_expert, n_used


def _rope_tables(s, extra_rows):
    rows = s // GRID_W
    axis_dim = HEAD_DIM // 2
    half = axis_dim // 2
    row = jnp.repeat(jnp.arange(rows, dtype=f32), GRID_W)
    col = jnp.tile(jnp.arange(GRID_W, dtype=f32), rows)
    inv = ROPE_THETA ** (-jnp.arange(0, axis_dim, 2, dtype=f32) / axis_dim)
    ang = jnp.concatenate([row[:, None] * inv, col[:, None] * inv], axis=-1)
    lane = jnp.arange(LANES)
    src = ((lane % HEAD_DIM) // axis_dim) * half + lane % half
    sign = jnp.where((lane % axis_dim) < half, -1.0, 1.0).astype(f32)
    cos_t = jnp.cos(ang)[:, src]
    sin_t = jnp.sin(ang)[:, src] * sign
    cos_t = jnp.concatenate([cos_t, jnp.ones((extra_rows, LANES), f32)], axis=0)
    sin_t = jnp.concatenate([sin_t, jnp.zeros((extra_rows, LANES), f32)], axis=0)
    return cos_t, sin_t


def kernel(x, c, ctx, c_ctx, w_ada, b_ada, g_norm_mix, g_norm_ffn, w_in, g_q, g_k, w_conf_dw, b_conf_dw,
           g_conf_ln, b_conf_ln, w_sc_dw, g_grp, w_out, w_router, b_router, w_e_in, b_e_in, w_e_out, b_e_out):
    b, s, d = x.shape
    n_ctx = ctx.shape[1]
    depth = w_ada.shape[0]
    n_experts = w_router.shape[2]
    d_exp = w_e_out.shape[2]
    attn_w = d // 2
    n_q = attn_w // HEAD_DIM
    n_kv = max(n_q // 4, 1)
    kv_w = n_kv * HEAD_DIM
    conf_w = d // 4
    sconv_w = d - attn_w - conf_w
    assert n_kv == 2 and kv_w == LANES and conf_w == sconv_w
    n_lat = b * s
    assert s % TOKEN_TILE == 0 and (b * n_ctx) % TOKEN_TILE == 0 and s % KEY_CHUNK == 0 and s % Q_TILE == 0

    perm = jnp.concatenate([jnp.concatenate([jnp.arange(HEAD_DIM) + j * HEAD_DIM,
                                             jnp.arange(HEAD_DIM) + (j + n_q // 2) * HEAD_DIM])
                            for j in range(n_q // 2)])
    col_perm = jnp.concatenate([perm, jnp.arange(attn_w, w_in.shape[2])])
    w_in_b = w_in[:, :, col_perm].astype(bf16)
    w_out_b = w_out[:, col_perm[:d], :].astype(bf16)
    g_grp_p = g_grp[:, col_perm[:d]]
    gq_t = jnp.tile(g_q, (1, n_q)).reshape(depth, 1, attn_w)
    gk_t = jnp.tile(g_k, (1, n_kv)).reshape(depth, 1, kv_w)
    cos_t, sin_t = _rope_tables(s, TOKEN_TILE)

    w1 = _weights_call(_w1_prep_kernel, w_e_in.reshape(depth * n_experts, d, 2 * d_exp), 512)
    w1 = w1.reshape(depth, n_experts, d, 2 * d_exp)
    w2 = _weights_call(_cast_kernel, w_e_out.reshape(depth * n_experts, d_exp, d), 512)
    w2 = w2.reshape(depth, n_experts, d_exp, d)
    b1 = b_e_in.reshape(depth, n_experts, 2 * d_exp // MXU_W, LANES, 2)
    b1 = jnp.swapaxes(b1, -1, -2).reshape(depth, n_experts, 1, 2 * d_exp)
    b2 = b_e_out.reshape(depth, n_experts, 1, d)
    wr = jnp.pad(w_router, ((0, 0), (0, 0), (0, LANES - n_experts))).astype(bf16)
    br = jnp.pad(b_router, ((0, 0), (0, LANES - n_experts)), constant_values=NEG_BIG).reshape(depth, 1, LANES)

    mod_rows = b + 1
    pad_rows = -mod_rows % 8
    cc = jnp.concatenate([c, c_ctx[None, :], jnp.zeros((pad_rows, d), f32)], axis=0)
    mods = _ada_call(cc, w_ada, b_ada)[:, :mod_rows].reshape(depth, mod_rows, N_ADA, d)

    xf = jnp.concatenate([x.reshape(n_lat, d), ctx.reshape(b * n_ctx, d)], axis=0)
    for l in range(depth):
        qt, k, vt, conf_in, sc_in = _inproj_call(
            xf, mods[l], g_norm_mix[l][None], w_in_b[l], gq_t[l], gk_t[l], cos_t, sin_t,
            b=b, s=s, attn_w=attn_w, kv_w=kv_w, conf_w2=2 * conf_w, sc_w3=3 * sconv_w)
        g_attn = g_grp_p[l][None, :attn_w]
        ya = _attn_call(qt, k, vt, g_attn, None, b=b, s=s, ctx=n_ctx, latent=True)
        ya = _attn_call(qt, k, vt, g_attn, ya, b=b, s=s, ctx=n_ctx, latent=False)
        yb = _conv_call(conf_in, sc_in, w_conf_dw[l], b_conf_dw[l][None], g_conf_ln[l][None], b_conf_ln[l][None],
                        w_sc_dw[l], g_grp[l][None, attn_w:attn_w + conf_w], g_grp[l][None, attn_w + conf_w:],
                        b=b, s=s, ctx=n_ctx)
        xf, h2, route, counts = _outproj_call(xf, ya, yb, w_out_b[l], mods[l], g_norm_ffn[l][None], wr[l], br[l],
                                              b=b, s=s)
        tok_buf, pos, blk_expert, n_used = _dispatch_plan(route, counts, n_experts)
        ys = _moe_call(blk_expert, n_used, h2[tok_buf], w1[l], b1[l], w2[l], b2[l])
        xf = _combine_call(xf, [ys[pos[:, kk]] for kk in range(TOP_K)], route, mods[l], b=b, s=s)
    return xf[:n_lat].reshape(b, s, d)
```

```python
import functools
import math

import jax
import jax.numpy as jnp
from jax import lax
from jax.experimental import pallas as pl
from jax.experimental.pallas import tpu as pltpu

GRID_W = 64
HEAD_DIM = 64
ROPE_THETA = 10000.0
TOP_K = 4
SWIGLU_ALPHA = 1.702
SWIGLU_LIMIT = 7.0
EPS = 1e-6
N_ADA = 6
LANES = 128
MXU_W = 256
NEG_BIG = -1e30
LOG2E = math.log2(math.e)

TOKEN_TILE = 512
Q_TILE = 256
KEY_CHUNK = 512
CONV_TILE = 256
CONF_HALO = 16
SC_HALO = 8
EXPERT_TILE = 512
MOE_PARTS = 2
WEIGHT_ROWS = 1024
ONES_ROWS = 16
VMEM_LIMIT = 48 * 1024 * 1024
ROUTE_IDX, ROUTE_PROB, ROUTE_RANK = 0, TOP_K, 2 * TOP_K

bf16 = jnp.bfloat16
f32 = jnp.float32


def _dot(a, b):
    return jnp.dot(a, b, preferred_element_type=f32)


def _sigmoid(x):
    return 1.0 / (1.0 + jnp.exp(-x))


def _split_bf16(x):
    hi = x.astype(bf16)
    lo = (x - hi.astype(f32)).astype(bf16)
    return hi, lo


def _params():
    return pltpu.CompilerParams(vmem_limit_bytes=VMEM_LIMIT)


def _ada_kernel(c_ref, w_ref, b_ref, o_ref):
    c = c_ref[...]
    s = c * _sigmoid(c)
    s_hi, s_lo = _split_bf16(s)
    w_hi, w_lo = _split_bf16(w_ref[0])
    o_ref[0] = _dot(s_hi, w_hi) + _dot(s_lo, w_hi) + _dot(s_hi, w_lo) + b_ref[0]


def _ada_call(cc, w_ada, b_ada):
    depth, d, n6 = w_ada.shape
    rows = cc.shape[0]
    tn = 1536
    return pl.pallas_call(
        _ada_kernel,
        grid=(depth, n6 // tn),
        in_specs=[pl.BlockSpec((rows, d), lambda l, j: (0, 0)),
                  pl.BlockSpec((1, d, tn), lambda l, j: (l, 0, j)),
                  pl.BlockSpec((1, 1, tn), lambda l, j: (l, 0, j))],
        out_specs=pl.BlockSpec((1, rows, tn), lambda l, j: (l, 0, j)),
        out_shape=jax.ShapeDtypeStruct((depth, rows, n6), f32),
        compiler_params=_params(),
    )(cc, w_ada, b_ada.reshape(depth, 1, n6))


def _w1_prep_kernel(w_ref, o_ref):
    r = lax.broadcasted_iota(jnp.int32, (MXU_W, MXU_W), 0)
    c = lax.broadcasted_iota(jnp.int32, (MXU_W, MXU_W), 1)
    src = jnp.where(c < LANES, 2 * c, 2 * (c - LANES) + 1)
    sel = jnp.where(r == src, 1.0, 0.0).astype(bf16)
    for grp in range(w_ref.shape[2] // MXU_W):
        cols = slice(grp * MXU_W, (grp + 1) * MXU_W)
        o_ref[0, :, cols] = _dot(w_ref[0, :, cols].astype(bf16), sel).astype(bf16)


def _cast_kernel(w_ref, o_ref):
    o_ref[...] = w_ref[...].astype(bf16)


def _weights_call(kern, w, rows):
    ne, d, n = w.shape
    return pl.pallas_call(
        kern,
        grid=(ne, d // rows),
        in_specs=[pl.BlockSpec((1, rows, n), lambda e, j: (e, j, 0))],
        out_specs=pl.BlockSpec((1, rows, n), lambda e, j: (e, j, 0)),
        out_shape=jax.ShapeDtypeStruct(w.shape, bf16),
        compiler_params=_params(),
    )(w)


def _modulated(x, g, mod, shift_row, scale_row):
    ms = jnp.mean(x * x, axis=-1, keepdims=True)
    xh = x * lax.rsqrt(ms + EPS)
    return xh * (g * (1.0 + mod[scale_row:scale_row + 1, :])) + mod[shift_row:shift_row + 1, :]


def _head_sumsq(y):
    n = y.shape[1]
    r = lax.broadcasted_iota(jnp.int32, (n, n), 0) // HEAD_DIM
    c = lax.broadcasted_iota(jnp.int32, (n, n), 1) // HEAD_DIM
    ones_bd = jnp.where(r == c, 1.0, 0.0).astype(bf16)
    return _dot((y * y).astype(bf16), ones_bd)


def _rope(y, cos, sin_signed):
    lane = lax.broadcasted_iota(jnp.int32, y.shape, 1)
    partner = jnp.where((lane % 32) < 16, pltpu.roll(y, LANES - 16, 1), pltpu.roll(y, 16, 1))
    return y * cos + partner * sin_signed


def _seg_index(i, tile, seg_len, n_seg):
    return jnp.minimum(i * tile // seg_len, n_seg)


def _inproj_kernel(x_ref, mod_ref, g_ref, w_ref, gq_ref, gk_ref, cos_ref, sin_ref,
                   qt_ref, k_ref, vt_ref, conf_ref, sc_ref, *, attn_w, kv_w, conf_w2):
    h = _modulated(x_ref[...], g_ref[...], mod_ref[0], 0, 1).astype(bf16)
    cos = cos_ref[...]
    sin = sin_ref[...]
    o_k = attn_w
    o_v = o_k + kv_w
    o_conf = o_v + kv_w
    o_sc = o_conf + conf_w2

    q = _dot(h, w_ref[:, 0:attn_w])
    for cb in range(attn_w // MXU_W):
        qb = q[:, cb * MXU_W:(cb + 1) * MXU_W]
        qn = qb * lax.rsqrt(_head_sumsq(qb) * (1.0 / HEAD_DIM) + EPS) * gq_ref[:, cb * MXU_W:(cb + 1) * MXU_W]
        for half in range(2):
            lo = cb * MXU_W + half * LANES
            y = _rope(qn[:, half * LANES:(half + 1) * LANES], cos, sin) * (LOG2E * HEAD_DIM ** -0.5)
            qt_ref[lo:lo + LANES, :] = y.T.astype(bf16)

    k = _dot(h, w_ref[:, o_k:o_v])
    kn = k * lax.rsqrt(_head_sumsq(k) * (1.0 / HEAD_DIM) + EPS) * gk_ref[...]
    k_ref[...] = _rope(kn, cos, sin).astype(bf16)

    vt = _dot(h, w_ref[:, o_v:o_conf]).T.astype(bf16)
    ones = jnp.ones((HEAD_DIM, vt.shape[1]), bf16)
    for hh in range(kv_w // HEAD_DIM):
        vt_ref[2 * hh * HEAD_DIM:(2 * hh + 1) * HEAD_DIM, :] = vt[hh * HEAD_DIM:(hh + 1) * HEAD_DIM, :]
        vt_ref[(2 * hh + 1) * HEAD_DIM:(2 * hh + 2) * HEAD_DIM, :] = ones

    conf_ref[...] = _dot(h, w_ref[:, o_conf:o_sc])
    sc_ref[...] = _dot(h, w_ref[:, o_sc:])


def _inproj_call(x, mods, g, w, gq, gk, cos_t, sin_t, *, b, s, attn_w, kv_w, conf_w2, sc_w3):
    n, d = x.shape
    tm = TOKEN_TILE
    n_lat_tiles = b * s // tm
    tiles_per_seq = s // tm
    in_w = w.shape[1]

    def tab_map(i):
        return (jnp.where(i < n_lat_tiles, i % tiles_per_seq, tiles_per_seq), 0)

    kern = functools.partial(_inproj_kernel, attn_w=attn_w, kv_w=kv_w, conf_w2=conf_w2)
    return pl.pallas_call(
        kern,
        grid=(n // tm,),
        in_specs=[pl.BlockSpec((tm, d), lambda i: (i, 0)),
                  pl.BlockSpec((1, N_ADA, d), lambda i: (_seg_index(i, tm, s, b), 0, 0)),
                  pl.BlockSpec((1, d), lambda i: (0, 0)),
                  pl.BlockSpec((d, in_w), lambda i: (0, 0)),
                  pl.BlockSpec((1, attn_w), lambda i: (0, 0)),
                  pl.BlockSpec((1, kv_w), lambda i: (0, 0)),
                  pl.BlockSpec((tm, LANES), tab_map),
                  pl.BlockSpec((tm, LANES), tab_map)],
        out_specs=[pl.BlockSpec((attn_w, tm), lambda i: (0, i)),
                   pl.BlockSpec((tm, kv_w), lambda i: (i, 0)),
                   pl.BlockSpec((2 * kv_w, tm), lambda i: (0, i)),
                   pl.BlockSpec((tm, conf_w2), lambda i: (i, 0)),
                   pl.BlockSpec((tm, sc_w3), lambda i: (i, 0))],
        out_shape=[jax.ShapeDtypeStruct((attn_w, n), bf16),
                   jax.ShapeDtypeStruct((n, kv_w), bf16),
                   jax.ShapeDtypeStruct((2 * kv_w, n), bf16),
                   jax.ShapeDtypeStruct((n, conf_w2), f32),
                   jax.ShapeDtypeStruct((n, sc_w3), f32)],
        compiler_params=_params(),
    )(x, mods, g, w, gq, gk, cos_t, sin_t)


def _attn_kernel(*refs, n_chunks1, chunk1, n_pairs, aliased):
    if aliased:
        refs = refs[1:]
    qt_ref, k1_ref, vt1_ref, k2_ref, vt2_ref, g_ref, o_ref, qs_ref, m_ref, acc_ref, st_ref = refs
    tq = qt_ref.shape[1]
    rows2 = k2_ref.shape[0]
    v_rows = HEAD_DIM + ONES_ROWS
    sub = lax.broadcasted_iota(jnp.int32, (LANES, tq), 0)
    for grp in range(2):
        in_group = (sub < HEAD_DIM) if grp == 0 else (sub >= HEAD_DIM)
        for j in range(n_pairs):
            qj = qt_ref[j * LANES:(j + 1) * LANES, :]
            qs_ref[grp, :, j * tq:(j + 1) * tq] = jnp.where(in_group, qj, jnp.zeros_like(qj))
    m_ref[...] = jnp.full(m_ref.shape, NEG_BIG, f32)
    acc_ref[...] = jnp.zeros(acc_ref.shape, f32)

    def produce(buf, kblk):
        for grp in range(2):
            st_ref[buf, grp, 0:kblk.shape[0], :] = _dot(kblk, qs_ref[grp])

    def consume(buf, rows, vt_of):
        for grp in range(2):
            st = st_ref[buf, grp, 0:rows, :]
            m_old = m_ref[grp]
            m_new = jnp.maximum(m_old, jnp.max(st, axis=0, keepdims=True))
            alpha = jnp.exp2(m_old - m_new)
            pt = jnp.exp2(st - m_new[0:1, :]).astype(bf16)
            acc_ref[grp] = acc_ref[grp] * alpha[0:1, :] + _dot(vt_of(grp), pt)
            m_ref[grp] = m_new

    def k1_at(c):
        return k1_ref[pl.ds(pl.multiple_of(c * chunk1, chunk1), chunk1), :]

    def vt1_at(c):
        start = pl.multiple_of(c * chunk1, chunk1)
        return lambda grp: vt1_ref[grp * LANES:grp * LANES + v_rows, pl.ds(start, chunk1)]

    vt2_of = lambda grp: vt2_ref[grp * LANES:grp * LANES + v_rows, :]

    if n_chunks1 > 0:
        assert n_chunks1 % 2 == 0
        produce(0, k1_at(0))

        def body(i, carry):
            produce(1, k1_at(2 * i + 1))
            consume(0, chunk1, vt1_at(2 * i))
            produce(0, k1_at(2 * i + 2))
            consume(1, chunk1, vt1_at(2 * i + 1))
            return carry
        lax.fori_loop(0, n_chunks1 // 2 - 1, body, 0)
        produce(1, k1_at(n_chunks1 - 1))
        consume(0, chunk1, vt1_at(n_chunks1 - 2))
        produce(0, k2_ref[...])
        consume(1, chunk1, vt1_at(n_chunks1 - 1))
    else:
        produce(0, k2_ref[...])
    consume(0, rows2, vt2_of)

    outs = []
    for grp in range(2):
        acc = acc_ref[grp]
        outs.append(acc[0:HEAD_DIM, :] / acc[HEAD_DIM:HEAD_DIM + 1, :])
    ot = jnp.concatenate([outs[grp][:, j * tq:(j + 1) * tq] for j in range(n_pairs) for grp in range(2)], axis=0)
    inv = lax.rsqrt(jnp.mean(ot * ot, axis=0, keepdims=True) + EPS)
    o_ref[...] = ((ot * inv).T * g_ref[...]).astype(bf16)


def _attn_call(qt, k, vt, g, prev, *, b, s, ctx, latent):
    attn_w, n = qt.shape
    n_pairs = attn_w // LANES
    kv_w = k.shape[1]
    ctx_blk0 = b * s // ctx
    if latent:
        tq = Q_TILE
        per_b = s // tq
        grid = (b, per_b)
        q_idx = lambda bi, j: bi * per_b + j
        kv1_rows, n_chunks1, chunk1 = s, s // KEY_CHUNK, KEY_CHUNK
        kv1_idx = lambda bi, j: bi
    else:
        tq = ctx
        grid = (b, 1)
        q_idx = lambda bi, j: ctx_blk0 + bi
        kv1_rows, n_chunks1, chunk1 = ctx, 0, ctx
        kv1_idx = q_idx
    kv2_idx = lambda bi, j: ctx_blk0 + bi
    aliased = prev is not None
    kern = functools.partial(_attn_kernel, n_chunks1=n_chunks1, chunk1=chunk1, n_pairs=n_pairs, aliased=aliased)
    in_specs = [pl.BlockSpec((attn_w, tq), lambda bi, j: (0, q_idx(bi, j))),
                pl.BlockSpec((kv1_rows, kv_w), lambda bi, j: (kv1_idx(bi, j), 0)),
                pl.BlockSpec((2 * kv_w, kv1_rows), lambda bi, j: (0, kv1_idx(bi, j))),
                pl.BlockSpec((ctx, kv_w), lambda bi, j: (kv2_idx(bi, j), 0)),
                pl.BlockSpec((2 * kv_w, ctx), lambda bi, j: (0, kv2_idx(bi, j))),
                pl.BlockSpec((1, attn_w), lambda bi, j: (0, 0))]
    args = [qt, k, vt, k, vt, g]
    if aliased:
        in_specs = [pl.BlockSpec(memory_space=pl.ANY)] + in_specs
        args = [prev] + args
    return pl.pallas_call(
        kern,
        grid=grid,
        in_specs=in_specs,
        out_specs=pl.BlockSpec((tq, attn_w), lambda bi, j: (q_idx(bi, j), 0)),
        out_shape=jax.ShapeDtypeStruct((n, attn_w), bf16),
        scratch_shapes=[pltpu.VMEM((2, LANES, n_pairs * tq), bf16),
                        pltpu.VMEM((2, 8, n_pairs * tq), f32),
                        pltpu.VMEM((2, HEAD_DIM + ONES_ROWS, n_pairs * tq), f32),
                        pltpu.VMEM((2, 2, max(chunk1, ctx), n_pairs * tq), f32)],
        input_output_aliases={0: 0} if aliased else {},
        compiler_params=_params(),
    )(*args)


def _conv_kernel(cp_ref, cc_ref, cn_ref, sp_ref, scc_ref, sn_ref, w31_ref, b31_ref, gln_ref, bln_ref,
                 w3_ref, gc_ref, gs_ref, o_ref, ext_ref, ext3_ref,
                 *, n_lat_tiles, tiles_per_seq, cw, taps31, taps3):
    i = pl.program_id(0)
    tc = cc_ref.shape[0]
    is_lat = i < n_lat_tiles
    left_edge = jnp.where(is_lat, (i % tiles_per_seq) == 0, True)
    right_edge = jnp.where(is_lat, (i % tiles_per_seq) == tiles_per_seq - 1, True)

    def glu(blk):
        return blk[:, 0:cw] * _sigmoid(blk[:, cw:2 * cw])

    ext_ref[0:CONF_HALO, :] = jnp.where(left_edge, 0.0, glu(cp_ref[...]))
    ext_ref[CONF_HALO:CONF_HALO + tc, :] = glu(cc_ref[...])
    ext_ref[CONF_HALO + tc:, :] = jnp.where(right_edge, 0.0, glu(cn_ref[...]))
    pad31 = (taps31 - 1) // 2
    u = jnp.zeros((tc, cw), f32) + b31_ref[...]
    for t in range(taps31):
        off = CONF_HALO - pad31 + t
        u = u + ext_ref[off:off + tc, :] * w31_ref[t:t + 1, :]
    mu = jnp.mean(u, axis=-1, keepdims=True)
    var = jnp.mean(jnp.square(u - mu), axis=-1, keepdims=True)
    y = (u - mu) * lax.rsqrt(var + EPS) * gln_ref[...] + bln_ref[...]
    conf = y * _sigmoid(y)

    def ch(blk):
        return blk[:, cw:2 * cw] * blk[:, 2 * cw:3 * cw]

    ext3_ref[0:SC_HALO, :] = jnp.where(left_edge, 0.0, ch(sp_ref[...]))
    ext3_ref[SC_HALO:SC_HALO + tc, :] = ch(scc_ref[...])
    ext3_ref[SC_HALO + tc:, :] = jnp.where(right_edge, 0.0, ch(sn_ref[...]))
    pad3 = (taps3 - 1) // 2
    acc = jnp.zeros((tc, cw), f32)
    for t in range(taps3):
        off = SC_HALO - pad3 + t
        acc = acc + ext3_ref[off:off + tc, :] * w3_ref[t:t + 1, :]
    sc = scc_ref[:, 0:cw] * acc

    def grp_norm(z, g_ref_):
        return z * lax.rsqrt(jnp.mean(z * z, axis=-1, keepdims=True) + EPS) * g_ref_[...]

    o_ref[:, 0:cw] = grp_norm(conf, gc_ref).astype(bf16)
    o_ref[:, cw:2 * cw] = grp_norm(sc, gs_ref).astype(bf16)


def _conv_call(conf_in, sc_in, w31, b31, gln, bln, w3, gc, gs, *, b, s, ctx):
    n = conf_in.shape[0]
    cw = w31.shape[1]
    tc = CONV_TILE
    assert ctx == tc and s % tc == 0
    n_tiles = n // tc
    ch_per = tc // CONF_HALO
    sh_per = tc // SC_HALO
    kern = functools.partial(_conv_kernel, n_lat_tiles=b * s // tc, tiles_per_seq=s // tc, cw=cw,
                             taps31=w31.shape[0], taps3=w3.shape[0])
    small = lambda a: pl.BlockSpec(a.shape, lambda i: (0, 0))
    return pl.pallas_call(
        kern,
        grid=(n_tiles,),
        in_specs=[pl.BlockSpec((CONF_HALO, 2 * cw), lambda i: (jnp.maximum(i * ch_per - 1, 0), 0)),
                  pl.BlockSpec((tc, 2 * cw), lambda i: (i, 0)),
                  pl.BlockSpec((CONF_HALO, 2 * cw), lambda i: (jnp.minimum((i + 1) * ch_per, n_tiles * ch_per - 1), 0)),
                  pl.BlockSpec((SC_HALO, 3 * cw), lambda i: (jnp.maximum(i * sh_per - 1, 0), 0)),
                  pl.BlockSpec((tc, 3 * cw), lambda i: (i, 0)),
                  pl.BlockSpec((SC_HALO, 3 * cw), lambda i: (jnp.minimum((i + 1) * sh_per, n_tiles * sh_per - 1), 0)),
                  small(w31), small(b31), small(gln), small(bln), small(w3), small(gc), small(gs)],
        out_specs=pl.BlockSpec((tc, 2 * cw), lambda i: (i, 0)),
        out_shape=jax.ShapeDtypeStruct((n, 2 * cw), bf16),
        scratch_shapes=[pltpu.VMEM((tc + 2 * CONF_HALO, cw), f32),
                        pltpu.VMEM((tc + 2 * SC_HALO, cw), f32)],
        compiler_params=_params(),
    )(conf_in, conf_in, conf_in, sc_in, sc_in, sc_in, w31, b31, gln, bln, w3, gc, gs)


def _outproj_kernel(x_ref, ya_ref, yb_ref, w_ref, mod_ref, g_ref, wr_ref, br_ref,
                    xo_ref, h_ref, route_ref, cnt_ref, *, attn_w):
    @pl.when(pl.program_id(0) == 0)
    def _():
        cnt_ref[...] = jnp.zeros(cnt_ref.shape, f32)

    mod = mod_ref[0]
    y = _dot(ya_ref[...], w_ref[0:attn_w, :]) + _dot(yb_ref[...], w_ref[attn_w:, :])
    xn = x_ref[...] + mod[2:3, :] * y
    xo_ref[...] = xn
    h = _modulated(xn, g_ref[...], mod, 3, 4).astype(bf16)
    h_ref[...] = h

    logits = _dot(h, wr_ref[...]) + br_ref[...]
    tm = logits.shape[0]
    lane = lax.broadcasted_iota(jnp.int32, logits.shape, 1).astype(f32)
    vals, idxs = [], []
    chosen = jnp.zeros(logits.shape, f32)
    for _ in range(TOP_K):
        m = jnp.max(logits, axis=-1, keepdims=True)
        first = jnp.min(jnp.where(logits == m, lane, float(LANES)), axis=-1, keepdims=True)
        vals.append(m)
        idxs.append(first)
        hit = lane == first
        chosen = jnp.where(hit, 1.0, chosen)
        logits = jnp.where(hit, NEG_BIG, logits)
    es = [jnp.exp(vv - vals[0]) for vv in vals]
    den = es[0] + es[1] + es[2] + es[3]

    r = lax.broadcasted_iota(jnp.int32, (tm, tm), 0)
    c = lax.broadcasted_iota(jnp.int32, (tm, tm), 1)
    before = _dot(jnp.where(r > c, 1.0, 0.0).astype(bf16), chosen.astype(bf16)) + cnt_ref[0:1, :]
    cnt_ref[...] = cnt_ref[...] + jnp.sum(chosen, axis=0, keepdims=True)

    route = jnp.zeros(logits.shape, f32)
    for k in range(TOP_K):
        rank = jnp.sum(jnp.where(lane == idxs[k], before, 0.0), axis=-1, keepdims=True)
        route = jnp.where(lane == float(ROUTE_IDX + k), idxs[k], route)
        route = jnp.where(lane == float(ROUTE_PROB + k), es[k] / den, route)
        route = jnp.where(lane == float(ROUTE_RANK + k), rank, route)
    route_ref[...] = route


def _outproj_call(x, ya, yb, w, mods, g, wr, br, *, b, s, tile0, n_tiles):
    n, d = x.shape
    tm = TOKEN_TILE
    attn_w = ya.shape[1]
    kern = functools.partial(_outproj_kernel, attn_w=attn_w)
    row = lambda i: (i + tile0, 0)
    local = lambda i: (i, 0)
    fixed = lambda i: (0, 0)
    return pl.pallas_call(
        kern,
        grid=(n_tiles,),
        in_specs=[pl.BlockSpec((tm, d), row),
                  pl.BlockSpec((tm, attn_w), row),
                  pl.BlockSpec((tm, yb.shape[1]), row),
                  pl.BlockSpec(w.shape, fixed),
                  pl.BlockSpec((1, N_ADA, d), lambda i: (_seg_index(i + tile0, tm, s, b), 0, 0)),
                  pl.BlockSpec((1, d), fixed),
                  pl.BlockSpec(wr.shape, fixed),
                  pl.BlockSpec((1, LANES), fixed)],
        out_specs=[pl.BlockSpec((tm, d), row), pl.BlockSpec((tm, d), local),
                   pl.BlockSpec((tm, LANES), local), pl.BlockSpec((8, LANES), fixed)],
        out_shape=[jax.ShapeDtypeStruct((n, d), f32), jax.ShapeDtypeStruct((n_tiles * tm, d), bf16),
                   jax.ShapeDtypeStruct((n_tiles * tm, LANES), f32), jax.ShapeDtypeStruct((8, LANES), f32)],
        input_output_aliases={0: 0},
        compiler_params=_params(),
    )(x, ya, yb, w, mods, g, wr, br)


def _moe_kernel(be_ref, nu_ref, xs_ref, w1_ref, b1_ref, w2_ref, b2_ref, ys_ref):
    @pl.when(pl.program_id(0) < nu_ref[0])
    def _():
        u = _dot(xs_ref[...], w1_ref[0]) + b1_ref[0]
        acts = []
        for grp in range(u.shape[1] // MXU_W):
            gate = jnp.minimum(u[:, grp * MXU_W:grp * MXU_W + LANES], SWIGLU_LIMIT)
            up = jnp.clip(u[:, grp * MXU_W + LANES:(grp + 1) * MXU_W], -SWIGLU_LIMIT, SWIGLU_LIMIT)
            acts.append(((up + 1.0) * (gate * _sigmoid(SWIGLU_ALPHA * gate))).astype(bf16))
        ys_ref[...] = (_dot(jnp.concatenate(acts, axis=1), w2_ref[0]) + b2_ref[0]).astype(bf16)


def _moe_call(blk_expert, n_used, xs, w1, b1, w2, b2):
    cap, d = xs.shape
    bm = EXPERT_TILE
    de2 = w1.shape[2]
    row = lambda i, be, nu: (jnp.minimum(i, nu[0] - 1), 0)
    exp3 = lambda i, be, nu: (be[i], 0, 0)
    return pl.pallas_call(
        _moe_kernel,
        grid_spec=pltpu.PrefetchScalarGridSpec(
            num_scalar_prefetch=2,
            grid=(cap // bm,),
            in_specs=[pl.BlockSpec((bm, d), row),
                      pl.BlockSpec((1, d, de2), exp3),
                      pl.BlockSpec((1, 1, de2), exp3),
                      pl.BlockSpec((1, de2 // 2, d), exp3),
                      pl.BlockSpec((1, 1, d), exp3)],
            out_specs=pl.BlockSpec((bm, d), row)),
        out_shape=jax.ShapeDtypeStruct((cap, d), bf16),
        compiler_params=_params(),
    )(blk_expert, n_used, xs, w1, b1, w2, b2)


def _combine_kernel(x_ref, y0_ref, y1_ref, y2_ref, y3_ref, route_ref, mod_ref, o_ref):
    route = route_ref[...]
    f = jnp.zeros(x_ref.shape, f32)
    for k, y_ref in enumerate((y0_ref, y1_ref, y2_ref, y3_ref)):
        f = f + route[:, ROUTE_PROB + k:ROUTE_PROB + k + 1] * y_ref[...].astype(f32)
    o_ref[...] = x_ref[...] + mod_ref[0][5:6, :] * f


def _combine_call(x, ys4, route, mods, *, b, s, tile0, n_tiles):
    n, d = x.shape
    tm = TOKEN_TILE
    row = lambda i: (i + tile0, 0)
    local = lambda i: (i, 0)
    return pl.pallas_call(
        _combine_kernel,
        grid=(n_tiles,),
        in_specs=[pl.BlockSpec((tm, d), row)] + [pl.BlockSpec((tm, d), local)] * TOP_K +
                 [pl.BlockSpec((tm, LANES), local),
                  pl.BlockSpec((1, N_ADA, d), lambda i: (_seg_index(i + tile0, tm, s, b), 0, 0))],
        out_specs=pl.BlockSpec((tm, d), row),
        out_shape=jax.ShapeDtypeStruct((n, d), f32),
        input_output_aliases={0: 0},
        compiler_params=_params(),
    )(x, *ys4, route, mods)


def _dispatch_plan(route, counts_f, n_experts):
    n_tok = route.shape[0]
    n_pairs = n_tok * TOP_K
    bm = EXPERT_TILE
    pair_bits = (n_pairs - 1).bit_length()
    assert pair_bits + (n_experts - 1).bit_length() < 31
    top_i = route[:, ROUTE_IDX:ROUTE_IDX + TOP_K].astype(jnp.int32)
    rank = route[:, ROUTE_RANK:ROUTE_RANK + TOP_K].astype(jnp.int32)
    counts = counts_f[0, :n_experts].astype(jnp.int32)
    starts = jnp.cumsum(counts) - counts
    padded = (counts + bm - 1) // bm * bm
    pad_ends = jnp.cumsum(padded)
    pad_starts = pad_ends - padded
    experts = jnp.arange(n_experts, dtype=jnp.int32)
    pos = jnp.sum(jnp.where(top_i[:, :, None] == experts, pad_starts, 0), axis=-1) + rank
    n_blocks = -(-n_pairs // bm) + n_experts
    blk_start = jnp.arange(n_blocks, dtype=jnp.int32) * bm
    blk_expert = jnp.minimum(jnp.sum(blk_start[:, None] >= pad_ends[None, :], axis=-1), n_experts - 1).astype(jnp.int32)
    n_used = (pad_ends[-1] // bm).astype(jnp.int32).reshape(1)
    pair_id = jnp.arange(n_pairs, dtype=jnp.int32)
    keys = jnp.sort(top_i.reshape(-1) * (1 << pair_bits) + pair_id)
    sorted_tok = (keys & ((1 << pair_bits) - 1)) // TOP_K
    shift = (starts - pad_starts)[blk_expert]
    src = jnp.clip((blk_start + shift)[:, None] + jnp.arange(bm, dtype=jnp.int32)[None, :], 0, n_pairs - 1)
    tok_buf = sorted_tok[src.reshape(-1)]
    return tok_buf, pos, blk_expert, n_used


def _rope_tables(s, extra_rows):
    rows = s // GRID_W
    axis_dim = HEAD_DIM // 2
    half = axis_dim // 2
    row = jnp.repeat(jnp.arange(rows, dtype=f32), GRID_W)
    col = jnp.tile(jnp.arange(GRID_W, dtype=f32), rows)
    inv = ROPE_THETA ** (-jnp.arange(0, axis_dim, 2, dtype=f32) / axis_dim)
    ang = jnp.concatenate([row[:, None] * inv, col[:, None] * inv], axis=-1)
    lane = jnp.arange(LANES)
    src = ((lane % HEAD_DIM) // axis_dim) * half + lane % half
    sign = jnp.where((lane % axis_dim) < half, -1.0, 1.0).astype(f32)
    cos_t = jnp.cos(ang)[:, src]
    sin_t = jnp.sin(ang)[:, src] * sign
    cos_t = jnp.concatenate([cos_t, jnp.ones((extra_rows, LANES), f32)], axis=0)
    sin_t = jnp.concatenate([sin_t, jnp.zeros((extra_rows, LANES), f32)], axis=0)
    return cos_t, sin_t


def kernel(x, c, ctx, c_ctx, w_ada, b_ada, g_norm_mix, g_norm_ffn, w_in, g_q, g_k, w_conf_dw, b_conf_dw,
           g_conf_ln, b_conf_ln, w_sc_dw, g_grp, w_out, w_router, b_router, w_e_in, b_e_in, w_e_out, b_e_out):
    b, s, d = x.shape
    n_ctx = ctx.shape[1]
    depth = w_ada.shape[0]
    n_experts = w_router.shape[2]
    d_exp = w_e_out.shape[2]
    attn_w = d // 2
    n_q = attn_w // HEAD_DIM
    n_kv = max(n_q // 4, 1)
    kv_w = n_kv * HEAD_DIM
    conf_w = d // 4
    sconv_w = d - attn_w - conf_w
    assert n_kv == 2 and kv_w == LANES and conf_w == sconv_w
    n_lat = b * s
    n_tiles = (n_lat + b * n_ctx) // TOKEN_TILE
    assert n_tiles % MOE_PARTS == 0
    part_tiles = n_tiles // MOE_PARTS
    assert s % TOKEN_TILE == 0 and (b * n_ctx) % TOKEN_TILE == 0 and s % KEY_CHUNK == 0 and s % Q_TILE == 0

    perm = jnp.concatenate([jnp.concatenate([jnp.arange(HEAD_DIM) + j * HEAD_DIM,
                                             jnp.arange(HEAD_DIM) + (j + n_q // 2) * HEAD_DIM])
                            for j in range(n_q // 2)])
    col_perm = jnp.concatenate([perm, jnp.arange(attn_w, w_in.shape[2])])
    w_in_b = w_in[:, :, col_perm].astype(bf16)
    w_out_b = w_out[:, col_perm[:d], :].astype(bf16)
    g_grp_p = g_grp[:, col_perm[:d]]
    gq_t = jnp.tile(g_q, (1, n_q)).reshape(depth, 1, attn_w)
    gk_t = jnp.tile(g_k, (1, n_kv)).reshape(depth, 1, kv_w)
    cos_t, sin_t = _rope_tables(s, TOKEN_TILE)

    w1 = _weights_call(_w1_prep_kernel, w_e_in.reshape(depth * n_experts, d, 2 * d_exp), WEIGHT_ROWS)
    w1 = w1.reshape(depth, n_experts, d, 2 * d_exp)
    w2 = _weights_call(_cast_kernel, w_e_out.reshape(depth * n_experts, d_exp, d), WEIGHT_ROWS)
    w2 = w2.reshape(depth, n_experts, d_exp, d)
    b1 = b_e_in.reshape(depth, n_experts, 2 * d_exp // MXU_W, LANES, 2)
    b1 = jnp.swapaxes(b1, -1, -2).reshape(depth, n_experts, 1, 2 * d_exp)
    b2 = b_e_out.reshape(depth, n_experts, 1, d)
    wr = jnp.pad(w_router, ((0, 0), (0, 0), (0, LANES - n_experts))).astype(bf16)
    br = jnp.pad(b_router, ((0, 0), (0, LANES - n_experts)), constant_values=NEG_BIG).reshape(depth, 1, LANES)

    mod_rows = b + 1
    pad_rows = -mod_rows % 8
    cc = jnp.concatenate([c, c_ctx[None, :], jnp.zeros((pad_rows, d), f32)], axis=0)
    mods = _ada_call(cc, w_ada, b_ada)[:, :mod_rows].reshape(depth, mod_rows, N_ADA, d)

    xf = jnp.concatenate([x.reshape(n_lat, d), ctx.reshape(b * n_ctx, d)], axis=0)
    for l in range(depth):
        qt, k, vt, conf_in, sc_in = _inproj_call(
            xf, mods[l], g_norm_mix[l][None], w_in_b[l], gq_t[l], gk_t[l], cos_t, sin_t,
            b=b, s=s, attn_w=attn_w, kv_w=kv_w, conf_w2=2 * conf_w, sc_w3=3 * sconv_w)
        g_attn = g_grp_p[l][None, :attn_w]
        ya = _attn_call(qt, k, vt, g_attn, None, b=b, s=s, ctx=n_ctx, latent=True)
        ya = _attn_call(qt, k, vt, g_attn, ya, b=b, s=s, ctx=n_ctx, latent=False)
        yb = _conv_call(conf_in, sc_in, w_conf_dw[l], b_conf_dw[l][None], g_conf_ln[l][None], b_conf_ln[l][None],
                        w_sc_dw[l], g_grp[l][None, attn_w:attn_w + conf_w], g_grp[l][None, attn_w + conf_w:],
                        b=b, s=s, ctx=n_ctx)
        parts = []
        for part in range(MOE_PARTS):
            xf, h2, route, counts = _outproj_call(xf, ya, yb, w_out_b[l], mods[l], g_norm_ffn[l][None], wr[l], br[l],
                                                  b=b, s=s, tile0=part * part_tiles, n_tiles=part_tiles)
            parts.append((h2, route, counts))
        gathered = []
        for h2, route, counts in parts:
            tok_buf, pos, blk_expert, n_used = _dispatch_plan(route, counts, n_experts)
            ys = _moe_call(blk_expert, n_used, h2[tok_buf], w1[l], b1[l], w2[l], b2[l])
            gathered.append([ys[pos[:, kk]] for kk in range(TOP_K)])
        for part in range(MOE_PARTS):
            xf = _combine_call(xf, gathered[part], parts[part][1], mods[l], b=b, s=s,
                               tile0=part * part_tiles, n_tiles=part_tiles)
    return xf[:n_lat].reshape(b, s, d)
```

```python
import functools
import math

import jax
import jax.numpy as jnp
from jax import lax
from jax.experimental import pallas as pl
from jax.experimental.pallas import tpu as pltpu

GRID_W = 64
HEAD_DIM = 64
ROPE_THETA = 10000.0
TOP_K = 4
SWIGLU_ALPHA = 1.702
SWIGLU_LIMIT = 7.0
EPS = 1e-6
N_ADA = 6
LANES = 128
SUBLANES = 8
MXU_W = 256
NEG_BIG = -1e30
LOG2E = math.log2(math.e)

TOKEN_TILE = 512
Q_TILE = 512
KEY_CHUNK = 512
SCORE_COLS = 256
CONV_TILE = 256
CONF_HALO = 16
SC_HALO = 8
EXPERT_TILE = 512
MOE_PARTS = 2
WEIGHT_ROWS = 1024
ONES_ROWS = 16
VMEM_LIMIT = 48 * 1024 * 1024
ROUTE_IDX, ROUTE_PROB, ROUTE_RANK = 0, TOP_K, 2 * TOP_K

bf16 = jnp.bfloat16
f32 = jnp.float32


def _dot(a, b):
    return jnp.dot(a, b, preferred_element_type=f32)


def _sigmoid(x):
    return 1.0 / (1.0 + jnp.exp(-x))


def _split_bf16(x):
    hi = x.astype(bf16)
    lo = (x - hi.astype(f32)).astype(bf16)
    return hi, lo


def _params():
    return pltpu.CompilerParams(vmem_limit_bytes=VMEM_LIMIT)


def _ada_kernel(c_ref, w_ref, b_ref, o_ref):
    c = c_ref[...]
    s = c * _sigmoid(c)
    s_hi, s_lo = _split_bf16(s)
    w_hi, w_lo = _split_bf16(w_ref[0])
    o_ref[0] = _dot(s_hi, w_hi) + _dot(s_lo, w_hi) + _dot(s_hi, w_lo) + b_ref[0]


def _ada_call(cc, w_ada, b_ada):
    depth, d, n6 = w_ada.shape
    rows = cc.shape[0]
    tn = 1536
    return pl.pallas_call(
        _ada_kernel,
        grid=(depth, n6 // tn),
        in_specs=[pl.BlockSpec((rows, d), lambda l, j: (0, 0)),
                  pl.BlockSpec((1, d, tn), lambda l, j: (l, 0, j)),
                  pl.BlockSpec((1, 1, tn), lambda l, j: (l, 0, j))],
        out_specs=pl.BlockSpec((1, rows, tn), lambda l, j: (l, 0, j)),
        out_shape=jax.ShapeDtypeStruct((depth, rows, n6), f32),
        compiler_params=_params(),
    )(cc, w_ada, b_ada.reshape(depth, 1, n6))


def _w1_prep_kernel(w_ref, o_ref):
    r = lax.broadcasted_iota(jnp.int32, (MXU_W, MXU_W), 0)
    c = lax.broadcasted_iota(jnp.int32, (MXU_W, MXU_W), 1)
    src = jnp.where(c < LANES, 2 * c, 2 * (c - LANES) + 1)
    sel = jnp.where(r == src, 1.0, 0.0).astype(bf16)
    for grp in range(w_ref.shape[2] // MXU_W):
        cols = slice(grp * MXU_W, (grp + 1) * MXU_W)
        o_ref[0, :, cols] = _dot(w_ref[0, :, cols].astype(bf16), sel).astype(bf16)


def _cast_kernel(w_ref, o_ref):
    o_ref[...] = w_ref[...].astype(bf16)


def _weights_call(kern, w, rows):
    ne, d, n = w.shape
    return pl.pallas_call(
        kern,
        grid=(ne, d // rows),
        in_specs=[pl.BlockSpec((1, rows, n), lambda e, j: (e, j, 0))],
        out_specs=pl.BlockSpec((1, rows, n), lambda e, j: (e, j, 0)),
        out_shape=jax.ShapeDtypeStruct(w.shape, bf16),
        compiler_params=_params(),
    )(w)


def _modulated(x, g, mod, shift_row, scale_row):
    ms = jnp.mean(x * x, axis=-1, keepdims=True)
    xh = x * lax.rsqrt(ms + EPS)
    return xh * (g * (1.0 + mod[scale_row:scale_row + 1, :])) + mod[shift_row:shift_row + 1, :]


def _head_sumsq(y):
    n = y.shape[1]
    r = lax.broadcasted_iota(jnp.int32, (n, n), 0) // HEAD_DIM
    c = lax.broadcasted_iota(jnp.int32, (n, n), 1) // HEAD_DIM
    ones_bd = jnp.where(r == c, 1.0, 0.0).astype(bf16)
    return _dot((y * y).astype(bf16), ones_bd)


def _rope(y, cos, sin_signed):
    lane = lax.broadcasted_iota(jnp.int32, y.shape, 1)
    partner = jnp.where((lane % 32) < 16, pltpu.roll(y, LANES - 16, 1), pltpu.roll(y, 16, 1))
    return y * cos + partner * sin_signed


def _seg_index(i, tile, seg_len, n_seg):
    return jnp.minimum(i * tile // seg_len, n_seg)


def _inproj_kernel(x_ref, mod_ref, g_ref, w_ref, gq_ref, gk_ref, cos_ref, sin_ref,
                   qt_ref, k_ref, vt_ref, conf_ref, sc_ref, *, attn_w, kv_w, conf_w2):
    h = _modulated(x_ref[...], g_ref[...], mod_ref[0], 0, 1).astype(bf16)
    cos = cos_ref[...]
    sin = sin_ref[...]
    o_k = attn_w
    o_v = o_k + kv_w
    o_conf = o_v + kv_w
    o_sc = o_conf + conf_w2

    q = _dot(h, w_ref[:, 0:attn_w])
    for cb in range(attn_w // MXU_W):
        qb = q[:, cb * MXU_W:(cb + 1) * MXU_W]
        qn = qb * lax.rsqrt(_head_sumsq(qb) * (1.0 / HEAD_DIM) + EPS) * gq_ref[:, cb * MXU_W:(cb + 1) * MXU_W]
        for half in range(2):
            lo = cb * MXU_W + half * LANES
            y = _rope(qn[:, half * LANES:(half + 1) * LANES], cos, sin) * (LOG2E * HEAD_DIM ** -0.5)
            qt_ref[lo:lo + LANES, :] = y.T.astype(bf16)

    k = _dot(h, w_ref[:, o_k:o_v])
    kn = k * lax.rsqrt(_head_sumsq(k) * (1.0 / HEAD_DIM) + EPS) * gk_ref[...]
    k_ref[...] = _rope(kn, cos, sin).astype(bf16)

    vt = _dot(h, w_ref[:, o_v:o_conf]).T.astype(bf16)
    ones = jnp.ones((HEAD_DIM, vt.shape[1]), bf16)
    for hh in range(kv_w // HEAD_DIM):
        vt_ref[2 * hh * HEAD_DIM:(2 * hh + 1) * HEAD_DIM, :] = vt[hh * HEAD_DIM:(hh + 1) * HEAD_DIM, :]
        vt_ref[(2 * hh + 1) * HEAD_DIM:(2 * hh + 2) * HEAD_DIM, :] = ones

    conf_ref[...] = _dot(h, w_ref[:, o_conf:o_sc])
    sc_ref[...] = _dot(h, w_ref[:, o_sc:])


def _inproj_call(x, mods, g, w, gq, gk, cos_t, sin_t, *, b, s, attn_w, kv_w, conf_w2, sc_w3):
    n, d = x.shape
    tm = TOKEN_TILE
    n_lat_tiles = b * s // tm
    tiles_per_seq = s // tm
    in_w = w.shape[1]

    def tab_map(i):
        return (jnp.where(i < n_lat_tiles, i % tiles_per_seq, tiles_per_seq), 0)

    kern = functools.partial(_inproj_kernel, attn_w=attn_w, kv_w=kv_w, conf_w2=conf_w2)
    return pl.pallas_call(
        kern,
        grid=(n // tm,),
        in_specs=[pl.BlockSpec((tm, d), lambda i: (i, 0)),
                  pl.BlockSpec((1, N_ADA, d), lambda i: (_seg_index(i, tm, s, b), 0, 0)),
                  pl.BlockSpec((1, d), lambda i: (0, 0)),
                  pl.BlockSpec((d, in_w), lambda i: (0, 0)),
                  pl.BlockSpec((1, attn_w), lambda i: (0, 0)),
                  pl.BlockSpec((1, kv_w), lambda i: (0, 0)),
                  pl.BlockSpec((tm, LANES), tab_map),
                  pl.BlockSpec((tm, LANES), tab_map)],
        out_specs=[pl.BlockSpec((attn_w, tm), lambda i: (0, i)),
                   pl.BlockSpec((tm, kv_w), lambda i: (i, 0)),
                   pl.BlockSpec((2 * kv_w, tm), lambda i: (0, i)),
                   pl.BlockSpec((tm, conf_w2), lambda i: (i, 0)),
                   pl.BlockSpec((tm, sc_w3), lambda i: (i, 0))],
        out_shape=[jax.ShapeDtypeStruct((attn_w, n), bf16),
                   jax.ShapeDtypeStruct((n, kv_w), bf16),
                   jax.ShapeDtypeStruct((2 * kv_w, n), bf16),
                   jax.ShapeDtypeStruct((n, conf_w2), f32),
                   jax.ShapeDtypeStruct((n, sc_w3), f32)],
        compiler_params=_params(),
    )(x, mods, g, w, gq, gk, cos_t, sin_t)


def _attn_kernel(*refs, n_chunks1, chunk1, n_pairs, aliased):
    if aliased:
        refs = refs[1:]
    qt_ref, k1_ref, vt1_ref, k2_ref, vt2_ref, g_ref, o_ref, qs_ref, m_ref, acc_ref, st_ref, mx_ref = refs
    tq = qt_ref.shape[1]
    rows2 = k2_ref.shape[0]
    v_rows = HEAD_DIM + ONES_ROWS
    for grp in range(2):
        for j in range(n_pairs):
            head = grp * n_pairs + j
            qh = qt_ref[head * HEAD_DIM:(head + 1) * HEAD_DIM, :]
            cols = slice(j * tq, (j + 1) * tq)
            qs_ref[grp, grp * HEAD_DIM:(grp + 1) * HEAD_DIM, cols] = qh
            qs_ref[grp, (1 - grp) * HEAD_DIM:(2 - grp) * HEAD_DIM, cols] = jnp.zeros_like(qh)
    m_ref[...] = jnp.full(m_ref.shape, NEG_BIG, f32)
    acc_ref[...] = jnp.zeros(acc_ref.shape, f32)
    n_sub = n_pairs * tq // SCORE_COLS

    def produce(buf, kblk, grp, sb):
        cols = slice(sb * SCORE_COLS, (sb + 1) * SCORE_COLS)
        st = _dot(kblk, qs_ref[grp, :, cols])
        st_ref[buf, grp, 0:kblk.shape[0], cols] = st
        mx_ref[buf, grp, :, cols] = jnp.broadcast_to(jnp.max(st, axis=0, keepdims=True), (8, SCORE_COLS))

    def consume(buf, rows, vt_of, grp, sb):
        cols = slice(sb * SCORE_COLS, (sb + 1) * SCORE_COLS)
        st = st_ref[buf, grp, 0:rows, cols]
        m_old = m_ref[grp, :, cols]
        m_new = jnp.maximum(m_old, mx_ref[buf, grp, :, cols])
        alpha = jnp.exp2(m_old - m_new)
        pt = jnp.exp2(st - m_new[0:1, :]).astype(bf16)
        acc_ref[grp, :, cols] = acc_ref[grp, :, cols] * alpha[0:1, :] + _dot(vt_of(grp), pt)
        m_ref[grp, :, cols] = m_new

    def stage(buf_next, k_next, buf_cur, rows, vt_of):
        for grp in range(2):
            for sb in range(n_sub):
                if k_next is not None:
                    produce(buf_next, k_next, grp, sb)
                if vt_of is not None:
                    consume(buf_cur, rows, vt_of, grp, sb)

    def k1_at(c):
        return k1_ref[pl.ds(pl.multiple_of(c * chunk1, chunk1), chunk1), :]

    def vt1_at(c):
        start = pl.multiple_of(c * chunk1, chunk1)
        return lambda grp: vt1_ref[grp * LANES:grp * LANES + v_rows, pl.ds(start, chunk1)]

    vt2_of = lambda grp: vt2_ref[grp * LANES:grp * LANES + v_rows, :]

    if n_chunks1 > 0:
        assert n_chunks1 % 2 == 0
        stage(0, k1_at(0), None, None, None)

        def body(i, carry):
            stage(1, k1_at(2 * i + 1), 0, chunk1, vt1_at(2 * i))
            stage(0, k1_at(2 * i + 2), 1, chunk1, vt1_at(2 * i + 1))
            return carry
        lax.fori_loop(0, n_chunks1 // 2 - 1, body, 0)
        stage(1, k1_at(n_chunks1 - 1), 0, chunk1, vt1_at(n_chunks1 - 2))
        stage(0, k2_ref[...], 1, chunk1, vt1_at(n_chunks1 - 1))
    else:
        stage(0, k2_ref[...], None, None, None)
    stage(None, None, 0, rows2, vt2_of)

    outs = []
    for grp in range(2):
        acc = acc_ref[grp]
        outs.append(acc[0:HEAD_DIM, :] / acc[HEAD_DIM:HEAD_DIM + 1, :])
    ot = jnp.concatenate([outs[grp][:, j * tq:(j + 1) * tq] for grp in range(2) for j in range(n_pairs)], axis=0)
    inv = lax.rsqrt(jnp.mean(ot * ot, axis=0, keepdims=True) + EPS)
    o_ref[...] = ((ot * inv).T * g_ref[...]).astype(bf16)


def _attn_call(qt, k, vt, g, prev, *, b, s, ctx, latent):
    attn_w, n = qt.shape
    n_pairs = attn_w // LANES
    kv_w = k.shape[1]
    ctx_blk0 = b * s // ctx
    if latent:
        tq = Q_TILE
        per_b = s // tq
        grid = (b, per_b)
        q_idx = lambda bi, j: bi * per_b + j
        kv1_rows, n_chunks1, chunk1 = s, s // KEY_CHUNK, KEY_CHUNK
        kv1_idx = lambda bi, j: bi
    else:
        tq = ctx
        grid = (b, 1)
        q_idx = lambda bi, j: ctx_blk0 + bi
        kv1_rows, n_chunks1, chunk1 = ctx, 0, ctx
        kv1_idx = q_idx
    kv2_idx = lambda bi, j: ctx_blk0 + bi
    aliased = prev is not None
    kern = functools.partial(_attn_kernel, n_chunks1=n_chunks1, chunk1=chunk1, n_pairs=n_pairs, aliased=aliased)
    in_specs = [pl.BlockSpec((attn_w, tq), lambda bi, j: (0, q_idx(bi, j))),
                pl.BlockSpec((kv1_rows, kv_w), lambda bi, j: (kv1_idx(bi, j), 0)),
                pl.BlockSpec((2 * kv_w, kv1_rows), lambda bi, j: (0, kv1_idx(bi, j))),
                pl.BlockSpec((ctx, kv_w), lambda bi, j: (kv2_idx(bi, j), 0)),
                pl.BlockSpec((2 * kv_w, ctx), lambda bi, j: (0, kv2_idx(bi, j))),
                pl.BlockSpec((1, attn_w), lambda bi, j: (0, 0))]
    args = [qt, k, vt, k, vt, g]
    if aliased:
        in_specs = [pl.BlockSpec(memory_space=pl.ANY)] + in_specs
        args = [prev] + args
    return pl.pallas_call(
        kern,
        grid=grid,
        in_specs=in_specs,
        out_specs=pl.BlockSpec((tq, attn_w), lambda bi, j: (q_idx(bi, j), 0)),
        out_shape=jax.ShapeDtypeStruct((n, attn_w), bf16),
        scratch_shapes=[pltpu.VMEM((2, LANES, n_pairs * tq), bf16),
                        pltpu.VMEM((2, 8, n_pairs * tq), f32),
                        pltpu.VMEM((2, HEAD_DIM + ONES_ROWS, n_pairs * tq), f32),
                        pltpu.VMEM((2, 2, max(chunk1, ctx), n_pairs * tq), f32),
                        pltpu.VMEM((2, 2, 8, n_pairs * tq), f32)],
        input_output_aliases={0: 0} if aliased else {},
        compiler_params=_params(),
    )(*args)


def _conv_kernel(cp_ref, cc_ref, cn_ref, sp_ref, scc_ref, sn_ref, w31_ref, b31_ref, gln_ref, bln_ref,
                 w3_ref, gc_ref, gs_ref, o_ref, ext_ref, ext3_ref, shift_ref,
                 *, n_lat_tiles, tiles_per_seq, cw, taps31, taps3):
    i = pl.program_id(0)
    tc = cc_ref.shape[0]
    is_lat = i < n_lat_tiles
    left_edge = jnp.where(is_lat, (i % tiles_per_seq) == 0, True)
    right_edge = jnp.where(is_lat, (i % tiles_per_seq) == tiles_per_seq - 1, True)

    def glu(blk):
        return blk[:, 0:cw] * _sigmoid(blk[:, cw:2 * cw])

    ext_ref[0:CONF_HALO, :] = jnp.where(left_edge, 0.0, glu(cp_ref[...]))
    ext_ref[CONF_HALO:CONF_HALO + tc, :] = glu(cc_ref[...])
    ext_ref[CONF_HALO + tc:, :] = jnp.where(right_edge, 0.0, glu(cn_ref[...]))
    pad31 = (taps31 - 1) // 2
    u = jnp.zeros((tc, cw), f32) + b31_ref[...]
    offs = [CONF_HALO - pad31 + t for t in range(taps31)]
    for res in range(SUBLANES):
        taps = [t for t in range(taps31) if offs[t] % SUBLANES == res]
        if not taps:
            continue
        span = max(offs[t] for t in taps) - res + tc
        shift_ref[res, 0:span, :] = ext_ref[res:res + span, :]
        for t in taps:
            lo = offs[t] - res
            u = u + shift_ref[res, lo:lo + tc, :] * w31_ref[t:t + 1, :]
    mu = jnp.mean(u, axis=-1, keepdims=True)
    var = jnp.mean(jnp.square(u - mu), axis=-1, keepdims=True)
    y = (u - mu) * lax.rsqrt(var + EPS) * gln_ref[...] + bln_ref[...]
    conf = y * _sigmoid(y)

    def ch(blk):
        return blk[:, cw:2 * cw] * blk[:, 2 * cw:3 * cw]

    ext3_ref[0:SC_HALO, :] = jnp.where(left_edge, 0.0, ch(sp_ref[...]))
    ext3_ref[SC_HALO:SC_HALO + tc, :] = ch(scc_ref[...])
    ext3_ref[SC_HALO + tc:, :] = jnp.where(right_edge, 0.0, ch(sn_ref[...]))
    pad3 = (taps3 - 1) // 2
    acc = jnp.zeros((tc, cw), f32)
    for t in range(taps3):
        off = SC_HALO - pad3 + t
        acc = acc + ext3_ref[off:off + tc, :] * w3_ref[t:t + 1, :]
    sc = scc_ref[:, 0:cw] * acc

    def grp_norm(z, g_ref_):
        return z * lax.rsqrt(jnp.mean(z * z, axis=-1, keepdims=True) + EPS) * g_ref_[...]

    o_ref[:, 0:cw] = grp_norm(conf, gc_ref).astype(bf16)
    o_ref[:, cw:2 * cw] = grp_norm(sc, gs_ref).astype(bf16)


def _conv_call(conf_in, sc_in, w31, b31, gln, bln, w3, gc, gs, *, b, s, ctx):
    n = conf_in.shape[0]
    cw = w31.shape[1]
    tc = CONV_TILE
    assert ctx == tc and s % tc == 0
    n_tiles = n // tc
    ch_per = tc // CONF_HALO
    sh_per = tc // SC_HALO
    kern = functools.partial(_conv_kernel, n_lat_tiles=b * s // tc, tiles_per_seq=s // tc, cw=cw,
                             taps31=w31.shape[0], taps3=w3.shape[0])
    small = lambda a: pl.BlockSpec(a.shape, lambda i: (0, 0))
    return pl.pallas_call(
        kern,
        grid=(n_tiles,),
        in_specs=[pl.BlockSpec((CONF_HALO, 2 * cw), lambda i: (jnp.maximum(i * ch_per - 1, 0), 0)),
                  pl.BlockSpec((tc, 2 * cw), lambda i: (i, 0)),
                  pl.BlockSpec((CONF_HALO, 2 * cw), lambda i: (jnp.minimum((i + 1) * ch_per, n_tiles * ch_per - 1), 0)),
                  pl.BlockSpec((SC_HALO, 3 * cw), lambda i: (jnp.maximum(i * sh_per - 1, 0), 0)),
                  pl.BlockSpec((tc, 3 * cw), lambda i: (i, 0)),
                  pl.BlockSpec((SC_HALO, 3 * cw), lambda i: (jnp.minimum((i + 1) * sh_per, n_tiles * sh_per - 1), 0)),
                  small(w31), small(b31), small(gln), small(bln), small(w3), small(gc), small(gs)],
        out_specs=pl.BlockSpec((tc, 2 * cw), lambda i: (i, 0)),
        out_shape=jax.ShapeDtypeStruct((n, 2 * cw), bf16),
        scratch_shapes=[pltpu.VMEM((tc + 2 * CONF_HALO, cw), f32),
                        pltpu.VMEM((tc + 2 * SC_HALO, cw), f32),
                        pltpu.VMEM((SUBLANES, tc + 2 * CONF_HALO, cw), f32)],
        compiler_params=_params(),
    )(conf_in, conf_in, conf_in, sc_in, sc_in, sc_in, w31, b31, gln, bln, w3, gc, gs)


def _outproj_kernel(x_ref, ya_ref, yb_ref, w_ref, mod_ref, g_ref, wr_ref, br_ref,
                    xo_ref, h_ref, route_ref, cnt_ref, *, attn_w):
    @pl.when(pl.program_id(0) == 0)
    def _():
        cnt_ref[...] = jnp.zeros(cnt_ref.shape, f32)

    mod = mod_ref[0]
    y = _dot(ya_ref[...], w_ref[0:attn_w, :]) + _dot(yb_ref[...], w_ref[attn_w:, :])
    xn = x_ref[...] + mod[2:3, :] * y
    xo_ref[...] = xn
    h = _modulated(xn, g_ref[...], mod, 3, 4).astype(bf16)
    h_ref[...] = h

    logits = _dot(h, wr_ref[...]) + br_ref[...]
    tm = logits.shape[0]
    lane = lax.broadcasted_iota(jnp.int32, logits.shape, 1).astype(f32)
    vals, idxs = [], []
    chosen = jnp.zeros(logits.shape, f32)
    for _ in range(TOP_K):
        m = jnp.max(logits, axis=-1, keepdims=True)
        first = jnp.min(jnp.where(logits == m, lane, float(LANES)), axis=-1, keepdims=True)
        vals.append(m)
        idxs.append(first)
        hit = lane == first
        chosen = jnp.where(hit, 1.0, chosen)
        logits = jnp.where(hit, NEG_BIG, logits)
    es = [jnp.exp(vv - vals[0]) for vv in vals]
    den = es[0] + es[1] + es[2] + es[3]

    r = lax.broadcasted_iota(jnp.int32, (tm, tm), 0)
    c = lax.broadcasted_iota(jnp.int32, (tm, tm), 1)
    before = _dot(jnp.where(r > c, 1.0, 0.0).astype(bf16), chosen.astype(bf16)) + cnt_ref[0:1, :]
    cnt_ref[...] = cnt_ref[...] + jnp.sum(chosen, axis=0, keepdims=True)

    route = jnp.zeros(logits.shape, f32)
    for k in range(TOP_K):
        rank = jnp.sum(jnp.where(lane == idxs[k], before, 0.0), axis=-1, keepdims=True)
        route = jnp.where(lane == float(ROUTE_IDX + k), idxs[k], route)
        route = jnp.where(lane == float(ROUTE_PROB + k), es[k] / den, route)
        route = jnp.where(lane == float(ROUTE_RANK + k), rank, route)
    route_ref[...] = route


def _outproj_call(x, ya, yb, w, mods, g, wr, br, *, b, s, tile0, n_tiles):
    n, d = x.shape
    tm = TOKEN_TILE
    attn_w = ya.shape[1]
    kern = functools.partial(_outproj_kernel, attn_w=attn_w)
    row = lambda i: (i + tile0, 0)
    local = lambda i: (i, 0)
    fixed = lambda i: (0, 0)
    return pl.pallas_call(
        kern,
        grid=(n_tiles,),
        in_specs=[pl.BlockSpec((tm, d), row),
                  pl.BlockSpec((tm, attn_w), row),
                  pl.BlockSpec((tm, yb.shape[1]), row),
                  pl.BlockSpec(w.shape, fixed),
                  pl.BlockSpec((1, N_ADA, d), lambda i: (_seg_index(i + tile0, tm, s, b), 0, 0)),
                  pl.BlockSpec((1, d), fixed),
                  pl.BlockSpec(wr.shape, fixed),
                  pl.BlockSpec((1, LANES), fixed)],
        out_specs=[pl.BlockSpec((tm, d), row), pl.BlockSpec((tm, d), local),
                   pl.BlockSpec((tm, LANES), local), pl.BlockSpec((8, LANES), fixed)],
        out_shape=[jax.ShapeDtypeStruct((n, d), f32), jax.ShapeDtypeStruct((n_tiles * tm, d), bf16),
                   jax.ShapeDtypeStruct((n_tiles * tm, LANES), f32), jax.ShapeDtypeStruct((8, LANES), f32)],
        input_output_aliases={0: 0},
        compiler_params=_params(),
    )(x, ya, yb, w, mods, g, wr, br)


def _moe_kernel(be_ref, nu_ref, xs_ref, w1_ref, b1_ref, w2_ref, b2_ref, ys_ref):
    @pl.when(pl.program_id(0) < nu_ref[0])
    def _():
        u = _dot(xs_ref[...], w1_ref[0]) + b1_ref[0]
        acts = []
        for grp in range(u.shape[1] // MXU_W):
            gate = jnp.minimum(u[:, grp * MXU_W:grp * MXU_W + LANES], SWIGLU_LIMIT)
            up = jnp.clip(u[:, grp * MXU_W + LANES:(grp + 1) * MXU_W], -SWIGLU_LIMIT, SWIGLU_LIMIT)
            acts.append(((up + 1.0) * (gate * _sigmoid(SWIGLU_ALPHA * gate))).astype(bf16))
        ys_ref[...] = (_dot(jnp.concatenate(acts, axis=1), w2_ref[0]) + b2_ref[0]).astype(bf16)


def _moe_call(blk_expert, n_used, xs, w1, b1, w2, b2):
    cap, d = xs.shape
    bm = EXPERT_TILE
    de2 = w1.shape[2]
    row = lambda i, be, nu: (jnp.minimum(i, nu[0] - 1), 0)
    exp3 = lambda i, be, nu: (be[i], 0, 0)
    return pl.pallas_call(
        _moe_kernel,
        grid_spec=pltpu.PrefetchScalarGridSpec(
            num_scalar_prefetch=2,
            grid=(cap // bm,),
            in_specs=[pl.BlockSpec((bm, d), row),
                      pl.BlockSpec((1, d, de2), exp3),
                      pl.BlockSpec((1, 1, de2), exp3),
                      pl.BlockSpec((1, de2 // 2, d), exp3),
                      pl.BlockSpec((1, 1, d), exp3)],
            out_specs=pl.BlockSpec((bm, d), row)),
        out_shape=jax.ShapeDtypeStruct((cap, d), bf16),
        compiler_params=_params(),
    )(blk_expert, n_used, xs, w1, b1, w2, b2)


def _combine_kernel(x_ref, y0_ref, y1_ref, y2_ref, y3_ref, route_ref, mod_ref, o_ref):
    route = route_ref[...]
    f = jnp.zeros(x_ref.shape, f32)
    for k, y_ref in enumerate((y0_ref, y1_ref, y2_ref, y3_ref)):
        f = f + route[:, ROUTE_PROB + k:ROUTE_PROB + k + 1] * y_ref[...].astype(f32)
    o_ref[...] = x_ref[...] + mod_ref[0][5:6, :] * f


def _combine_call(x, ys4, route, mods, *, b, s, tile0, n_tiles):
    n, d = x.shape
    tm = TOKEN_TILE
    row = lambda i: (i + tile0, 0)
    local = lambda i: (i, 0)
    return pl.pallas_call(
        _combine_kernel,
        grid=(n_tiles,),
        in_specs=[pl.BlockSpec((tm, d), row)] + [pl.BlockSpec((tm, d), local)] * TOP_K +
                 [pl.BlockSpec((tm, LANES), local),
                  pl.BlockSpec((1, N_ADA, d), lambda i: (_seg_index(i + tile0, tm, s, b), 0, 0))],
        out_specs=pl.BlockSpec((tm, d), row),
        out_shape=jax.ShapeDtypeStruct((n, d), f32),
        input_output_aliases={0: 0},
        compiler_params=_params(),
    )(x, *ys4, route, mods)


def _dispatch_plan(route, counts_f, n_experts):
    n_tok = route.shape[0]
    n_pairs = n_tok * TOP_K
    bm = EXPERT_TILE
    pair_bits = (n_pairs - 1).bit_length()
    assert pair_bits + (n_experts - 1).bit_length() < 31
    top_i = route[:, ROUTE_IDX:ROUTE_IDX + TOP_K].astype(jnp.int32)
    rank = route[:, ROUTE_RANK:ROUTE_RANK + TOP_K].astype(jnp.int32)
    counts = counts_f[0, :n_experts].astype(jnp.int32)
    starts = jnp.cumsum(counts) - counts
    padded = (counts + bm - 1) // bm * bm
    pad_ends = jnp.cumsum(padded)
    pad_starts = pad_ends - padded
    experts = jnp.arange(n_experts, dtype=jnp.int32)
    pos = jnp.sum(jnp.where(top_i[:, :, None] == experts, pad_starts, 0), axis=-1) + rank
    n_blocks = -(-n_pairs // bm) + n_experts
    blk_start = jnp.arange(n_blocks, dtype=jnp.int32) * bm
    blk_expert = jnp.minimum(jnp.sum(blk_start[:, None] >= pad_ends[None, :], axis=-1), n_experts - 1).astype(jnp.int32)
    n_used = (pad_ends[-1] // bm).astype(jnp.int32).reshape(1)
    pair_id = jnp.arange(n_pairs, dtype=jnp.int32)
    keys = jnp.sort(top_i.reshape(-1) * (1 << pair_bits) + pair_id)
    sorted_tok = (keys & ((1 << pair_bits) - 1)) // TOP_K
    shift = (starts - pad_starts)[blk_expert]
    src = jnp.clip((blk_start + shift)[:, None] + jnp.arange(bm, dtype=jnp.int32)[None, :], 0, n_pairs - 1)
    tok_buf = sorted_tok[src.reshape(-1)]
    return tok_buf, pos, blk_expert, n_used


def _rope_tables(s, extra_rows):
    rows = s // GRID_W
    axis_dim = HEAD_DIM // 2
    half = axis_dim // 2
    row = jnp.repeat(jnp.arange(rows, dtype=f32), GRID_W)
    col = jnp.tile(jnp.arange(GRID_W, dtype=f32), rows)
    inv = ROPE_THETA ** (-jnp.arange(0, axis_dim, 2, dtype=f32) / axis_dim)
    ang = jnp.concatenate([row[:, None] * inv, col[:, None] * inv], axis=-1)
    lane = jnp.arange(LANES)
    src = ((lane % HEAD_DIM) // axis_dim) * half + lane % half
    sign = jnp.where((lane % axis_dim) < half, -1.0, 1.0).astype(f32)
    cos_t = jnp.cos(ang)[:, src]
    sin_t = jnp.sin(ang)[:, src] * sign
    cos_t = jnp.concatenate([cos_t, jnp.ones((extra_rows, LANES), f32)], axis=0)
    sin_t = jnp.concatenate([sin_t, jnp.zeros((extra_rows, LANES), f32)], axis=0)
    return cos_t, sin_t


def kernel(x, c, ctx, c_ctx, w_ada, b_ada, g_norm_mix, g_norm_ffn, w_in, g_q, g_k, w_conf_dw, b_conf_dw,
           g_conf_ln, b_conf_ln, w_sc_dw, g_grp, w_out, w_router, b_router, w_e_in, b_e_in, w_e_out, b_e_out):
    b, s, d = x.shape
    n_ctx = ctx.shape[1]
    depth = w_ada.shape[0]
    n_experts = w_router.shape[2]
    d_exp = w_e_out.shape[2]
    attn_w = d // 2
    n_q = attn_w // HEAD_DIM
    n_kv = max(n_q // 4, 1)
    kv_w = n_kv * HEAD_DIM
    conf_w = d // 4
    sconv_w = d - attn_w - conf_w
    assert n_kv == 2 and kv_w == LANES and conf_w == sconv_w
    n_lat = b * s
    n_tiles = (n_lat + b * n_ctx) // TOKEN_TILE
    assert n_tiles % MOE_PARTS == 0
    part_tiles = n_tiles // MOE_PARTS
    assert s % TOKEN_TILE == 0 and (b * n_ctx) % TOKEN_TILE == 0 and s % KEY_CHUNK == 0 and s % Q_TILE == 0

    w_in_b = w_in.astype(bf16)
    w_out_b = w_out.astype(bf16)
    gq_t = jnp.tile(g_q, (1, n_q)).reshape(depth, 1, attn_w)
    gk_t = jnp.tile(g_k, (1, n_kv)).reshape(depth, 1, kv_w)
    cos_t, sin_t = _rope_tables(s, TOKEN_TILE)

    w1 = _weights_call(_w1_prep_kernel, w_e_in.reshape(depth * n_experts, d, 2 * d_exp), WEIGHT_ROWS)
    w1 = w1.reshape(depth, n_experts, d, 2 * d_exp)
    w2 = _weights_call(_cast_kernel, w_e_out.reshape(depth * n_experts, d_exp, d), WEIGHT_ROWS)
    w2 = w2.reshape(depth, n_experts, d_exp, d)
    b1 = b_e_in.reshape(depth, n_experts, 2 * d_exp // MXU_W, LANES, 2)
    b1 = jnp.swapaxes(b1, -1, -2).reshape(depth, n_experts, 1, 2 * d_exp)
    b2 = b_e_out.reshape(depth, n_experts, 1, d)
    wr = jnp.pad(w_router, ((0, 0), (0, 0), (0, LANES - n_experts))).astype(bf16)
    br = jnp.pad(b_router, ((0, 0), (0, LANES - n_experts)), constant_values=NEG_BIG).reshape(depth, 1, LANES)

    mod_rows = b + 1
    pad_rows = -mod_rows % 8
    cc = jnp.concatenate([c, c_ctx[None, :], jnp.zeros((pad_rows, d), f32)], axis=0)
    mods = _ada_call(cc, w_ada, b_ada)[:, :mod_rows].reshape(depth, mod_rows, N_ADA, d)

    xf = jnp.concatenate([x.reshape(n_lat, d), ctx.reshape(b * n_ctx, d)], axis=0)
    for l in range(depth):
        qt, k, vt, conf_in, sc_in = _inproj_call(
            xf, mods[l], g_norm_mix[l][None], w_in_b[l], gq_t[l], gk_t[l], cos_t, sin_t,
            b=b, s=s, attn_w=attn_w, kv_w=kv_w, conf_w2=2 * conf_w, sc_w3=3 * sconv_w)
        g_attn = g_grp[l][None, :attn_w]
        ya = _attn_call(qt, k, vt, g_attn, None, b=b, s=s, ctx=n_ctx, latent=True)
        ya = _attn_call(qt, k, vt, g_attn, ya, b=b, s=s, ctx=n_ctx, latent=False)
        yb = _conv_call(conf_in, sc_in, w_conf_dw[l], b_conf_dw[l][None], g_conf_ln[l][None], b_conf_ln[l][None],
                        w_sc_dw[l], g_grp[l][None, attn_w:attn_w + conf_w], g_grp[l][None, attn_w + conf_w:],
                        b=b, s=s, ctx=n_ctx)
        parts = []
        for part in range(MOE_PARTS):
            xf, h2, route, counts = _outproj_call(xf, ya, yb, w_out_b[l], mods[l], g_norm_ffn[l][None], wr[l], br[l],
                                                  b=b, s=s, tile0=part * part_tiles, n_tiles=part_tiles)
            parts.append((h2, route, counts))
        gathered = []
        for h2, route, counts in parts:
            tok_buf, pos, blk_expert, n_used = _dispatch_plan(route, counts, n_experts)
            ys = _moe_call(blk_expert, n_used, h2[tok_buf], w1[l], b1[l], w2[l], b2[l])
            gathered.append([ys[pos[:, kk]] for kk in range(TOP_K)])
        for part in range(MOE_PARTS):
            xf = _combine_call(xf, gathered[part], parts[part][1], mods[l], b=b, s=s,
                               tile0=part * part_tiles, n_tiles=part_tiles)
    return xf[:n_lat].reshape(b, s, d)
```

```python
import functools
import math

import jax
import jax.numpy as jnp
from jax import lax
from jax.experimental import pallas as pl
from jax.experimental.pallas import tpu as pltpu

GRID_W = 64
HEAD_DIM = 64
ROPE_THETA = 10000.0
TOP_K = 4
SWIGLU_ALPHA = 1.702
SWIGLU_LIMIT = 7.0
EPS = 1e-6
N_ADA = 6
LANES = 128
SUBLANES = 8
MXU_W = 256
NEG_BIG = -1e30
LOG2E = math.log2(math.e)

TOKEN_TILE = 512
Q_TILE = 512
KEY_CHUNK = 512
SCORE_COLS = 256
CONV_TILE = 256
CONF_HALO = 16
SC_HALO = 8
EXPERT_TILE = 512
MOE_PARTS = 3
WEIGHT_ROWS = 1024
ONES_ROWS = 16
VMEM_LIMIT = 48 * 1024 * 1024
ROUTE_IDX, ROUTE_PROB, ROUTE_RANK = 0, TOP_K, 2 * TOP_K

bf16 = jnp.bfloat16
f32 = jnp.float32


def _dot(a, b):
    return jnp.dot(a, b, preferred_element_type=f32)


def _sigmoid(x):
    return 1.0 / (1.0 + jnp.exp(-x))


def _split_bf16(x):
    hi = x.astype(bf16)
    lo = (x - hi.astype(f32)).astype(bf16)
    return hi, lo


def _params():
    return pltpu.CompilerParams(vmem_limit_bytes=VMEM_LIMIT)


def _ada_kernel(c_ref, w_ref, b_ref, o_ref):
    c = c_ref[...]
    s = c * _sigmoid(c)
    s_hi, s_lo = _split_bf16(s)
    w_hi, w_lo = _split_bf16(w_ref[0])
    o_ref[0] = _dot(s_hi, w_hi) + _dot(s_lo, w_hi) + _dot(s_hi, w_lo) + b_ref[0]


def _ada_call(cc, w_ada, b_ada):
    depth, d, n6 = w_ada.shape
    rows = cc.shape[0]
    tn = 1536
    return pl.pallas_call(
        _ada_kernel,
        grid=(depth, n6 // tn),
        in_specs=[pl.BlockSpec((rows, d), lambda l, j: (0, 0)),
                  pl.BlockSpec((1, d, tn), lambda l, j: (l, 0, j)),
                  pl.BlockSpec((1, 1, tn), lambda l, j: (l, 0, j))],
        out_specs=pl.BlockSpec((1, rows, tn), lambda l, j: (l, 0, j)),
        out_shape=jax.ShapeDtypeStruct((depth, rows, n6), f32),
        compiler_params=_params(),
    )(cc, w_ada, b_ada.reshape(depth, 1, n6))


def _w1_prep_kernel(w_ref, o_ref):
    r = lax.broadcasted_iota(jnp.int32, (MXU_W, MXU_W), 0)
    c = lax.broadcasted_iota(jnp.int32, (MXU_W, MXU_W), 1)
    src = jnp.where(c < LANES, 2 * c, 2 * (c - LANES) + 1)
    sel = jnp.where(r == src, 1.0, 0.0).astype(bf16)
    for grp in range(w_ref.shape[2] // MXU_W):
        cols = slice(grp * MXU_W, (grp + 1) * MXU_W)
        o_ref[0, :, cols] = _dot(w_ref[0, :, cols].astype(bf16), sel).astype(bf16)


def _cast_kernel(w_ref, o_ref):
    o_ref[...] = w_ref[...].astype(bf16)


def _weights_call(kern, w, rows):
    ne, d, n = w.shape
    return pl.pallas_call(
        kern,
        grid=(ne, d // rows),
        in_specs=[pl.BlockSpec((1, rows, n), lambda e, j: (e, j, 0))],
        out_specs=pl.BlockSpec((1, rows, n), lambda e, j: (e, j, 0)),
        out_shape=jax.ShapeDtypeStruct(w.shape, bf16),
        compiler_params=_params(),
    )(w)


def _modulated(x, g, mod, shift_row, scale_row):
    ms = jnp.mean(x * x, axis=-1, keepdims=True)
    xh = x * lax.rsqrt(ms + EPS)
    return xh * (g * (1.0 + mod[scale_row:scale_row + 1, :])) + mod[shift_row:shift_row + 1, :]


def _head_sumsq(y):
    n = y.shape[1]
    r = lax.broadcasted_iota(jnp.int32, (n, n), 0) // HEAD_DIM
    c = lax.broadcasted_iota(jnp.int32, (n, n), 1) // HEAD_DIM
    ones_bd = jnp.where(r == c, 1.0, 0.0).astype(bf16)
    return _dot((y * y).astype(bf16), ones_bd)


def _rope(y, cos, sin_signed):
    lane = lax.broadcasted_iota(jnp.int32, y.shape, 1)
    partner = jnp.where((lane % 32) < 16, pltpu.roll(y, LANES - 16, 1), pltpu.roll(y, 16, 1))
    return y * cos + partner * sin_signed


def _seg_index(i, tile, seg_len, n_seg):
    return jnp.minimum(i * tile // seg_len, n_seg)


def _inproj_kernel(x_ref, mod_ref, g_ref, w_ref, gq_ref, gk_ref, cos_ref, sin_ref,
                   qt_ref, k_ref, vt_ref, conf_ref, sc_ref, *, attn_w, kv_w, conf_w2):
    h = _modulated(x_ref[...], g_ref[...], mod_ref[0], 0, 1).astype(bf16)
    cos = cos_ref[...]
    sin = sin_ref[...]
    o_k = attn_w
    o_v = o_k + kv_w
    o_conf = o_v + kv_w
    o_sc = o_conf + conf_w2

    q = _dot(h, w_ref[:, 0:attn_w])
    for cb in range(attn_w // MXU_W):
        qb = q[:, cb * MXU_W:(cb + 1) * MXU_W]
        qn = qb * lax.rsqrt(_head_sumsq(qb) * (1.0 / HEAD_DIM) + EPS) * gq_ref[:, cb * MXU_W:(cb + 1) * MXU_W]
        for half in range(2):
            lo = cb * MXU_W + half * LANES
            y = _rope(qn[:, half * LANES:(half + 1) * LANES], cos, sin) * (LOG2E * HEAD_DIM ** -0.5)
            qt_ref[lo:lo + LANES, :] = y.T.astype(bf16)

    k = _dot(h, w_ref[:, o_k:o_v])
    kn = k * lax.rsqrt(_head_sumsq(k) * (1.0 / HEAD_DIM) + EPS) * gk_ref[...]
    k_ref[...] = _rope(kn, cos, sin).astype(bf16)

    vt = _dot(h, w_ref[:, o_v:o_conf]).T.astype(bf16)
    ones = jnp.ones((HEAD_DIM, vt.shape[1]), bf16)
    for hh in range(kv_w // HEAD_DIM):
        vt_ref[2 * hh * HEAD_DIM:(2 * hh + 1) * HEAD_DIM, :] = vt[hh * HEAD_DIM:(hh + 1) * HEAD_DIM, :]
        vt_ref[(2 * hh + 1) * HEAD_DIM:(2 * hh + 2) * HEAD_DIM, :] = ones

    conf_ref[...] = _dot(h, w_ref[:, o_conf:o_sc])
    sc_ref[...] = _dot(h, w_ref[:, o_sc:])


def _inproj_call(x, mods, g, w, gq, gk, cos_t, sin_t, *, b, s, attn_w, kv_w, conf_w2, sc_w3):
    n, d = x.shape
    tm = TOKEN_TILE
    n_lat_tiles = b * s // tm
    tiles_per_seq = s // tm
    in_w = w.shape[1]

    def tab_map(i):
        return (jnp.where(i < n_lat_tiles, i % tiles_per_seq, tiles_per_seq), 0)

    kern = functools.partial(_inproj_kernel, attn_w=attn_w, kv_w=kv_w, conf_w2=conf_w2)
    return pl.pallas_call(
        kern,
        grid=(n // tm,),
        in_specs=[pl.BlockSpec((tm, d), lambda i: (i, 0)),
                  pl.BlockSpec((1, N_ADA, d), lambda i: (_seg_index(i, tm, s, b), 0, 0)),
                  pl.BlockSpec((1, d), lambda i: (0, 0)),
                  pl.BlockSpec((d, in_w), lambda i: (0, 0)),
                  pl.BlockSpec((1, attn_w), lambda i: (0, 0)),
                  pl.BlockSpec((1, kv_w), lambda i: (0, 0)),
                  pl.BlockSpec((tm, LANES), tab_map),
                  pl.BlockSpec((tm, LANES), tab_map)],
        out_specs=[pl.BlockSpec((attn_w, tm), lambda i: (0, i)),
                   pl.BlockSpec((tm, kv_w), lambda i: (i, 0)),
                   pl.BlockSpec((2 * kv_w, tm), lambda i: (0, i)),
                   pl.BlockSpec((tm, conf_w2), lambda i: (i, 0)),
                   pl.BlockSpec((tm, sc_w3), lambda i: (i, 0))],
        out_shape=[jax.ShapeDtypeStruct((attn_w, n), bf16),
                   jax.ShapeDtypeStruct((n, kv_w), bf16),
                   jax.ShapeDtypeStruct((2 * kv_w, n), bf16),
                   jax.ShapeDtypeStruct((n, conf_w2), f32),
                   jax.ShapeDtypeStruct((n, sc_w3), f32)],
        compiler_params=_params(),
    )(x, mods, g, w, gq, gk, cos_t, sin_t)


def _attn_kernel(*refs, n_chunks1, chunk1, n_pairs, aliased):
    if aliased:
        refs = refs[1:]
    qt_ref, k1_ref, vt1_ref, k2_ref, vt2_ref, g_ref, o_ref, qs_ref, m_ref, acc_ref, st_ref, mx_ref = refs
    tq = qt_ref.shape[1]
    rows2 = k2_ref.shape[0]
    v_rows = HEAD_DIM + ONES_ROWS
    for grp in range(2):
        for j in range(n_pairs):
            head = grp * n_pairs + j
            qh = qt_ref[head * HEAD_DIM:(head + 1) * HEAD_DIM, :]
            cols = slice(j * tq, (j + 1) * tq)
            qs_ref[grp, grp * HEAD_DIM:(grp + 1) * HEAD_DIM, cols] = qh
            qs_ref[grp, (1 - grp) * HEAD_DIM:(2 - grp) * HEAD_DIM, cols] = jnp.zeros_like(qh)
    m_ref[...] = jnp.full(m_ref.shape, NEG_BIG, f32)
    acc_ref[...] = jnp.zeros(acc_ref.shape, f32)
    n_sub = n_pairs * tq // SCORE_COLS

    def produce(buf, kblk, grp, sb):
        cols = slice(sb * SCORE_COLS, (sb + 1) * SCORE_COLS)
        st = _dot(kblk, qs_ref[grp, :, cols])
        st_ref[buf, grp, 0:kblk.shape[0], cols] = st
        mx_ref[buf, grp, :, cols] = jnp.broadcast_to(jnp.max(st, axis=0, keepdims=True), (8, SCORE_COLS))

    def consume(buf, rows, vt_of, grp, sb):
        cols = slice(sb * SCORE_COLS, (sb + 1) * SCORE_COLS)
        st = st_ref[buf, grp, 0:rows, cols]
        m_old = m_ref[grp, :, cols]
        m_new = jnp.maximum(m_old, mx_ref[buf, grp, :, cols])
        alpha = jnp.exp2(m_old - m_new)
        pt = jnp.exp2(st - m_new[0:1, :]).astype(bf16)
        acc_ref[grp, :, cols] = acc_ref[grp, :, cols] * alpha[0:1, :] + _dot(vt_of(grp), pt)
        m_ref[grp, :, cols] = m_new

    def stage(buf_next, k_next, buf_cur, rows, vt_of):
        for grp in range(2):
            for sb in range(n_sub):
                if k_next is not None:
                    produce(buf_next, k_next, grp, sb)
                if vt_of is not None:
                    consume(buf_cur, rows, vt_of, grp, sb)

    def k1_at(c):
        return k1_ref[pl.ds(pl.multiple_of(c * chunk1, chunk1), chunk1), :]

    def vt1_at(c):
        start = pl.multiple_of(c * chunk1, chunk1)
        return lambda grp: vt1_ref[grp * LANES:grp * LANES + v_rows, pl.ds(start, chunk1)]

    vt2_of = lambda grp: vt2_ref[grp * LANES:grp * LANES + v_rows, :]

    if n_chunks1 > 0:
        assert n_chunks1 % 2 == 0
        stage(0, k1_at(0), None, None, None)

        def body(i, carry):
            stage(1, k1_at(2 * i + 1), 0, chunk1, vt1_at(2 * i))
            stage(0, k1_at(2 * i + 2), 1, chunk1, vt1_at(2 * i + 1))
            return carry
        lax.fori_loop(0, n_chunks1 // 2 - 1, body, 0)
        stage(1, k1_at(n_chunks1 - 1), 0, chunk1, vt1_at(n_chunks1 - 2))
        stage(0, k2_ref[...], 1, chunk1, vt1_at(n_chunks1 - 1))
    else:
        stage(0, k2_ref[...], None, None, None)
    stage(None, None, 0, rows2, vt2_of)

    outs = []
    for grp in range(2):
        acc = acc_ref[grp]
        outs.append(acc[0:HEAD_DIM, :] / acc[HEAD_DIM:HEAD_DIM + 1, :])
    ot = jnp.concatenate([outs[grp][:, j * tq:(j + 1) * tq] for grp in range(2) for j in range(n_pairs)], axis=0)
    inv = lax.rsqrt(jnp.mean(ot * ot, axis=0, keepdims=True) + EPS)
    o_ref[...] = ((ot * inv).T * g_ref[...]).astype(bf16)


def _attn_call(qt, k, vt, g, prev, *, b, s, ctx, latent):
    attn_w, n = qt.shape
    n_pairs = attn_w // LANES
    kv_w = k.shape[1]
    ctx_blk0 = b * s // ctx
    if latent:
        tq = Q_TILE
        per_b = s // tq
        grid = (b, per_b)
        q_idx = lambda bi, j: bi * per_b + j
        kv1_rows, n_chunks1, chunk1 = s, s // KEY_CHUNK, KEY_CHUNK
        kv1_idx = lambda bi, j: bi
    else:
        tq = ctx
        grid = (b, 1)
        q_idx = lambda bi, j: ctx_blk0 + bi
        kv1_rows, n_chunks1, chunk1 = ctx, 0, ctx
        kv1_idx = q_idx
    kv2_idx = lambda bi, j: ctx_blk0 + bi
    aliased = prev is not None
    kern = functools.partial(_attn_kernel, n_chunks1=n_chunks1, chunk1=chunk1, n_pairs=n_pairs, aliased=aliased)
    in_specs = [pl.BlockSpec((attn_w, tq), lambda bi, j: (0, q_idx(bi, j))),
                pl.BlockSpec((kv1_rows, kv_w), lambda bi, j: (kv1_idx(bi, j), 0)),
                pl.BlockSpec((2 * kv_w, kv1_rows), lambda bi, j: (0, kv1_idx(bi, j))),
                pl.BlockSpec((ctx, kv_w), lambda bi, j: (kv2_idx(bi, j), 0)),
                pl.BlockSpec((2 * kv_w, ctx), lambda bi, j: (0, kv2_idx(bi, j))),
                pl.BlockSpec((1, attn_w), lambda bi, j: (0, 0))]
    args = [qt, k, vt, k, vt, g]
    if aliased:
        in_specs = [pl.BlockSpec(memory_space=pl.ANY)] + in_specs
        args = [prev] + args
    return pl.pallas_call(
        kern,
        grid=grid,
        in_specs=in_specs,
        out_specs=pl.BlockSpec((tq, attn_w), lambda bi, j: (q_idx(bi, j), 0)),
        out_shape=jax.ShapeDtypeStruct((n, attn_w), bf16),
        scratch_shapes=[pltpu.VMEM((2, LANES, n_pairs * tq), bf16),
                        pltpu.VMEM((2, 8, n_pairs * tq), f32),
                        pltpu.VMEM((2, HEAD_DIM + ONES_ROWS, n_pairs * tq), f32),
                        pltpu.VMEM((2, 2, max(chunk1, ctx), n_pairs * tq), f32),
                        pltpu.VMEM((2, 2, 8, n_pairs * tq), f32)],
        input_output_aliases={0: 0} if aliased else {},
        compiler_params=_params(),
    )(*args)


def _conv_kernel(cp_ref, cc_ref, cn_ref, sp_ref, scc_ref, sn_ref, w31_ref, b31_ref, gln_ref, bln_ref,
                 w3_ref, gc_ref, gs_ref, o_ref, ext_ref, ext3_ref, shift_ref,
                 *, n_lat_tiles, tiles_per_seq, cw, taps31, taps3):
    i = pl.program_id(0)
    tc = cc_ref.shape[0]
    is_lat = i < n_lat_tiles
    left_edge = jnp.where(is_lat, (i % tiles_per_seq) == 0, True)
    right_edge = jnp.where(is_lat, (i % tiles_per_seq) == tiles_per_seq - 1, True)

    def glu(blk):
        return blk[:, 0:cw] * _sigmoid(blk[:, cw:2 * cw])

    ext_ref[0:CONF_HALO, :] = jnp.where(left_edge, 0.0, glu(cp_ref[...]))
    ext_ref[CONF_HALO:CONF_HALO + tc, :] = glu(cc_ref[...])
    ext_ref[CONF_HALO + tc:, :] = jnp.where(right_edge, 0.0, glu(cn_ref[...]))
    pad31 = (taps31 - 1) // 2
    u = jnp.zeros((tc, cw), f32) + b31_ref[...]
    offs = [CONF_HALO - pad31 + t for t in range(taps31)]
    for res in range(SUBLANES):
        taps = [t for t in range(taps31) if offs[t] % SUBLANES == res]
        if not taps:
            continue
        span = max(offs[t] for t in taps) - res + tc
        shift_ref[res, 0:span, :] = ext_ref[res:res + span, :]
        for t in taps:
            lo = offs[t] - res
            u = u + shift_ref[res, lo:lo + tc, :] * w31_ref[t:t + 1, :]
    mu = jnp.mean(u, axis=-1, keepdims=True)
    var = jnp.mean(jnp.square(u - mu), axis=-1, keepdims=True)
    y = (u - mu) * lax.rsqrt(var + EPS) * gln_ref[...] + bln_ref[...]
    conf = y * _sigmoid(y)

    def ch(blk):
        return blk[:, cw:2 * cw] * blk[:, 2 * cw:3 * cw]

    ext3_ref[0:SC_HALO, :] = jnp.where(left_edge, 0.0, ch(sp_ref[...]))
    ext3_ref[SC_HALO:SC_HALO + tc, :] = ch(scc_ref[...])
    ext3_ref[SC_HALO + tc:, :] = jnp.where(right_edge, 0.0, ch(sn_ref[...]))
    pad3 = (taps3 - 1) // 2
    acc = jnp.zeros((tc, cw), f32)
    for t in range(taps3):
        off = SC_HALO - pad3 + t
        acc = acc + ext3_ref[off:off + tc, :] * w3_ref[t:t + 1, :]
    sc = scc_ref[:, 0:cw] * acc

    def grp_norm(z, g_ref_):
        return z * lax.rsqrt(jnp.mean(z * z, axis=-1, keepdims=True) + EPS) * g_ref_[...]

    o_ref[:, 0:cw] = grp_norm(conf, gc_ref).astype(bf16)
    o_ref[:, cw:2 * cw] = grp_norm(sc, gs_ref).astype(bf16)


def _conv_call(conf_in, sc_in, w31, b31, gln, bln, w3, gc, gs, *, b, s, ctx):
    n = conf_in.shape[0]
    cw = w31.shape[1]
    tc = CONV_TILE
    assert ctx == tc and s % tc == 0
    n_tiles = n // tc
    ch_per = tc // CONF_HALO
    sh_per = tc // SC_HALO
    kern = functools.partial(_conv_kernel, n_lat_tiles=b * s // tc, tiles_per_seq=s // tc, cw=cw,
                             taps31=w31.shape[0], taps3=w3.shape[0])
    small = lambda a: pl.BlockSpec(a.shape, lambda i: (0, 0))
    return pl.pallas_call(
        kern,
        grid=(n_tiles,),
        in_specs=[pl.BlockSpec((CONF_HALO, 2 * cw), lambda i: (jnp.maximum(i * ch_per - 1, 0), 0)),
                  pl.BlockSpec((tc, 2 * cw), lambda i: (i, 0)),
                  pl.BlockSpec((CONF_HALO, 2 * cw), lambda i: (jnp.minimum((i + 1) * ch_per, n_tiles * ch_per - 1), 0)),
                  pl.BlockSpec((SC_HALO, 3 * cw), lambda i: (jnp.maximum(i * sh_per - 1, 0), 0)),
                  pl.BlockSpec((tc, 3 * cw), lambda i: (i, 0)),
                  pl.BlockSpec((SC_HALO, 3 * cw), lambda i: (jnp.minimum((i + 1) * sh_per, n_tiles * sh_per - 1), 0)),
                  small(w31), small(b31), small(gln), small(bln), small(w3), small(gc), small(gs)],
        out_specs=pl.BlockSpec((tc, 2 * cw), lambda i: (i, 0)),
        out_shape=jax.ShapeDtypeStruct((n, 2 * cw), bf16),
        scratch_shapes=[pltpu.VMEM((tc + 2 * CONF_HALO, cw), f32),
                        pltpu.VMEM((tc + 2 * SC_HALO, cw), f32),
                        pltpu.VMEM((SUBLANES, tc + 2 * CONF_HALO, cw), f32)],
        compiler_params=_params(),
    )(conf_in, conf_in, conf_in, sc_in, sc_in, sc_in, w31, b31, gln, bln, w3, gc, gs)


def _outproj_kernel(x_ref, ya_ref, yb_ref, w_ref, mod_ref, g_ref, wr_ref, br_ref,
                    xo_ref, h_ref, route_ref, cnt_ref, *, attn_w):
    @pl.when(pl.program_id(0) == 0)
    def _():
        cnt_ref[...] = jnp.zeros(cnt_ref.shape, f32)

    mod = mod_ref[0]
    y = _dot(ya_ref[...], w_ref[0:attn_w, :]) + _dot(yb_ref[...], w_ref[attn_w:, :])
    xn = x_ref[...] + mod[2:3, :] * y
    xo_ref[...] = xn
    h = _modulated(xn, g_ref[...], mod, 3, 4).astype(bf16)
    h_ref[...] = h

    logits = _dot(h, wr_ref[...]) + br_ref[...]
    tm = logits.shape[0]
    lane = lax.broadcasted_iota(jnp.int32, logits.shape, 1).astype(f32)
    vals, idxs = [], []
    chosen = jnp.zeros(logits.shape, f32)
    for _ in range(TOP_K):
        m = jnp.max(logits, axis=-1, keepdims=True)
        first = jnp.min(jnp.where(logits == m, lane, float(LANES)), axis=-1, keepdims=True)
        vals.append(m)
        idxs.append(first)
        hit = lane == first
        chosen = jnp.where(hit, 1.0, chosen)
        logits = jnp.where(hit, NEG_BIG, logits)
    es = [jnp.exp(vv - vals[0]) for vv in vals]
    den = es[0] + es[1] + es[2] + es[3]

    r = lax.broadcasted_iota(jnp.int32, (tm, tm), 0)
    c = lax.broadcasted_iota(jnp.int32, (tm, tm), 1)
    before = _dot(jnp.where(r > c, 1.0, 0.0).astype(bf16), chosen.astype(bf16)) + cnt_ref[0:1, :]
    cnt_ref[...] = cnt_ref[...] + jnp.sum(chosen, axis=0, keepdims=True)

    route = jnp.zeros(logits.shape, f32)
    for k in range(TOP_K):
        rank = jnp.sum(jnp.where(lane == idxs[k], before, 0.0), axis=-1, keepdims=True)
        route = jnp.where(lane == float(ROUTE_IDX + k), idxs[k], route)
        route = jnp.where(lane == float(ROUTE_PROB + k), es[k] / den, route)
        route = jnp.where(lane == float(ROUTE_RANK + k), rank, route)
    route_ref[...] = route


def _outproj_call(x, ya, yb, w, mods, g, wr, br, *, b, s, tile0, n_tiles):
    n, d = x.shape
    tm = TOKEN_TILE
    attn_w = ya.shape[1]
    kern = functools.partial(_outproj_kernel, attn_w=attn_w)
    row = lambda i: (i + tile0, 0)
    local = lambda i: (i, 0)
    fixed = lambda i: (0, 0)
    return pl.pallas_call(
        kern,
        grid=(n_tiles,),
        in_specs=[pl.BlockSpec((tm, d), row),
                  pl.BlockSpec((tm, attn_w), row),
                  pl.BlockSpec((tm, yb.shape[1]), row),
                  pl.BlockSpec(w.shape, fixed),
                  pl.BlockSpec((1, N_ADA, d), lambda i: (_seg_index(i + tile0, tm, s, b), 0, 0)),
                  pl.BlockSpec((1, d), fixed),
                  pl.BlockSpec(wr.shape, fixed),
                  pl.BlockSpec((1, LANES), fixed)],
        out_specs=[pl.BlockSpec((tm, d), row), pl.BlockSpec((tm, d), local),
                   pl.BlockSpec((tm, LANES), local), pl.BlockSpec((8, LANES), fixed)],
        out_shape=[jax.ShapeDtypeStruct((n, d), f32), jax.ShapeDtypeStruct((n_tiles * tm, d), bf16),
                   jax.ShapeDtypeStruct((n_tiles * tm, LANES), f32), jax.ShapeDtypeStruct((8, LANES), f32)],
        input_output_aliases={0: 0},
        compiler_params=_params(),
    )(x, ya, yb, w, mods, g, wr, br)


def _moe_kernel(be_ref, nu_ref, xs_ref, w1_ref, b1_ref, w2_ref, b2_ref, ys_ref):
    @pl.when(pl.program_id(0) < nu_ref[0])
    def _():
        u = _dot(xs_ref[...], w1_ref[0]) + b1_ref[0]
        acts = []
        for grp in range(u.shape[1] // MXU_W):
            gate = jnp.minimum(u[:, grp * MXU_W:grp * MXU_W + LANES], SWIGLU_LIMIT)
            up = jnp.clip(u[:, grp * MXU_W + LANES:(grp + 1) * MXU_W], -SWIGLU_LIMIT, SWIGLU_LIMIT)
            acts.append(((up + 1.0) * (gate * _sigmoid(SWIGLU_ALPHA * gate))).astype(bf16))
        ys_ref[...] = (_dot(jnp.concatenate(acts, axis=1), w2_ref[0]) + b2_ref[0]).astype(bf16)


def _moe_call(blk_expert, n_used, xs, w1, b1, w2, b2):
    cap, d = xs.shape
    bm = EXPERT_TILE
    de2 = w1.shape[2]
    row = lambda i, be, nu: (jnp.minimum(i, nu[0] - 1), 0)
    exp3 = lambda i, be, nu: (be[i], 0, 0)
    return pl.pallas_call(
        _moe_kernel,
        grid_spec=pltpu.PrefetchScalarGridSpec(
            num_scalar_prefetch=2,
            grid=(cap // bm,),
            in_specs=[pl.BlockSpec((bm, d), row),
                      pl.BlockSpec((1, d, de2), exp3),
                      pl.BlockSpec((1, 1, de2), exp3),
                      pl.BlockSpec((1, de2 // 2, d), exp3),
                      pl.BlockSpec((1, 1, d), exp3)],
            out_specs=pl.BlockSpec((bm, d), row)),
        out_shape=jax.ShapeDtypeStruct((cap, d), bf16),
        compiler_params=_params(),
    )(blk_expert, n_used, xs, w1, b1, w2, b2)


def _combine_kernel(x_ref, y0_ref, y1_ref, y2_ref, y3_ref, route_ref, mod_ref, o_ref):
    route = route_ref[...]
    f = jnp.zeros(x_ref.shape, f32)
    for k, y_ref in enumerate((y0_ref, y1_ref, y2_ref, y3_ref)):
        f = f + route[:, ROUTE_PROB + k:ROUTE_PROB + k + 1] * y_ref[...].astype(f32)
    o_ref[...] = x_ref[...] + mod_ref[0][5:6, :] * f


def _combine_call(x, ys4, route, mods, *, b, s, tile0, n_tiles):
    n, d = x.shape
    tm = TOKEN_TILE
    row = lambda i: (i + tile0, 0)
    local = lambda i: (i, 0)
    return pl.pallas_call(
        _combine_kernel,
        grid=(n_tiles,),
        in_specs=[pl.BlockSpec((tm, d), row)] + [pl.BlockSpec((tm, d), local)] * TOP_K +
                 [pl.BlockSpec((tm, LANES), local),
                  pl.BlockSpec((1, N_ADA, d), lambda i: (_seg_index(i + tile0, tm, s, b), 0, 0))],
        out_specs=pl.BlockSpec((tm, d), row),
        out_shape=jax.ShapeDtypeStruct((n, d), f32),
        input_output_aliases={0: 0},
        compiler_params=_params(),
    )(x, *ys4, route, mods)


def _dispatch_plan(route, counts_f, n_experts):
    n_tok = route.shape[0]
    n_pairs = n_tok * TOP_K
    bm = EXPERT_TILE
    pair_bits = (n_pairs - 1).bit_length()
    assert pair_bits + (n_experts - 1).bit_length() < 31
    top_i = route[:, ROUTE_IDX:ROUTE_IDX + TOP_K].astype(jnp.int32)
    rank = route[:, ROUTE_RANK:ROUTE_RANK + TOP_K].astype(jnp.int32)
    counts = counts_f[0, :n_experts].astype(jnp.int32)
    starts = jnp.cumsum(counts) - counts
    padded = (counts + bm - 1) // bm * bm
    pad_ends = jnp.cumsum(padded)
    pad_starts = pad_ends - padded
    experts = jnp.arange(n_experts, dtype=jnp.int32)
    pos = jnp.sum(jnp.where(top_i[:, :, None] == experts, pad_starts, 0), axis=-1) + rank
    n_blocks = -(-n_pairs // bm) + n_experts
    blk_start = jnp.arange(n_blocks, dtype=jnp.int32) * bm
    blk_expert = jnp.minimum(jnp.sum(blk_start[:, None] >= pad_ends[None, :], axis=-1), n_experts - 1).astype(jnp.int32)
    n_used = (pad_ends[-1] // bm).astype(jnp.int32).reshape(1)
    pair_id = jnp.arange(n_pairs, dtype=jnp.int32)
    keys = jnp.sort(top_i.reshape(-1) * (1 << pair_bits) + pair_id)
    sorted_tok = (keys & ((1 << pair_bits) - 1)) // TOP_K
    shift = (starts - pad_starts)[blk_expert]
    src = jnp.clip((blk_start + shift)[:, None] + jnp.arange(bm, dtype=jnp.int32)[None, :], 0, n_pairs - 1)
    tok_buf = sorted_tok[src.reshape(-1)]
    return tok_buf, pos, blk_expert, n_used


def _rope_tables(s, extra_rows):
    rows = s // GRID_W
    axis_dim = HEAD_DIM // 2
    half = axis_dim // 2
    row = jnp.repeat(jnp.arange(rows, dtype=f32), GRID_W)
    col = jnp.tile(jnp.arange(GRID_W, dtype=f32), rows)
    inv = ROPE_THETA ** (-jnp.arange(0, axis_dim, 2, dtype=f32) / axis_dim)
    ang = jnp.concatenate([row[:, None] * inv, col[:, None] * inv], axis=-1)
    lane = jnp.arange(LANES)
    src = ((lane % HEAD_DIM) // axis_dim) * half + lane % half
    sign = jnp.where((lane % axis_dim) < half, -1.0, 1.0).astype(f32)
    cos_t = jnp.cos(ang)[:, src]
    sin_t = jnp.sin(ang)[:, src] * sign
    cos_t = jnp.concatenate([cos_t, jnp.ones((extra_rows, LANES), f32)], axis=0)
    sin_t = jnp.concatenate([sin_t, jnp.zeros((extra_rows, LANES), f32)], axis=0)
    return cos_t, sin_t


def kernel(x, c, ctx, c_ctx, w_ada, b_ada, g_norm_mix, g_norm_ffn, w_in, g_q, g_k, w_conf_dw, b_conf_dw,
           g_conf_ln, b_conf_ln, w_sc_dw, g_grp, w_out, w_router, b_router, w_e_in, b_e_in, w_e_out, b_e_out):
    b, s, d = x.shape
    n_ctx = ctx.shape[1]
    depth = w_ada.shape[0]
    n_experts = w_router.shape[2]
    d_exp = w_e_out.shape[2]
    attn_w = d // 2
    n_q = attn_w // HEAD_DIM
    n_kv = max(n_q // 4, 1)
    kv_w = n_kv * HEAD_DIM
    conf_w = d // 4
    sconv_w = d - attn_w - conf_w
    assert n_kv == 2 and kv_w == LANES and conf_w == sconv_w
    n_lat = b * s
    n_tiles = (n_lat + b * n_ctx) // TOKEN_TILE
    part_sizes = [n_tiles // MOE_PARTS + (1 if p < n_tiles % MOE_PARTS else 0) for p in range(MOE_PARTS)]
    part_ranges = [(sum(part_sizes[:p]), part_sizes[p]) for p in range(MOE_PARTS)]
    assert s % TOKEN_TILE == 0 and (b * n_ctx) % TOKEN_TILE == 0 and s % KEY_CHUNK == 0 and s % Q_TILE == 0

    w_in_b = w_in.astype(bf16)
    w_out_b = w_out.astype(bf16)
    gq_t = jnp.tile(g_q, (1, n_q)).reshape(depth, 1, attn_w)
    gk_t = jnp.tile(g_k, (1, n_kv)).reshape(depth, 1, kv_w)
    cos_t, sin_t = _rope_tables(s, TOKEN_TILE)

    w1 = _weights_call(_w1_prep_kernel, w_e_in.reshape(depth * n_experts, d, 2 * d_exp), WEIGHT_ROWS)
    w2 = _weights_call(_cast_kernel, w_e_out.reshape(depth * n_experts, d_exp, d), WEIGHT_ROWS)
    b1 = b_e_in.reshape(depth * n_experts, 2 * d_exp // MXU_W, LANES, 2)
    b1 = jnp.swapaxes(b1, -1, -2).reshape(depth * n_experts, 1, 2 * d_exp)
    b2 = b_e_out.reshape(depth * n_experts, 1, d)
    wr = jnp.pad(w_router, ((0, 0), (0, 0), (0, LANES - n_experts))).astype(bf16)
    br = jnp.pad(b_router, ((0, 0), (0, LANES - n_experts)), constant_values=NEG_BIG).reshape(depth, 1, LANES)

    mod_rows = b + 1
    pad_rows = -mod_rows % 8
    cc = jnp.concatenate([c, c_ctx[None, :], jnp.zeros((pad_rows, d), f32)], axis=0)
    mods = _ada_call(cc, w_ada, b_ada)[:, :mod_rows].reshape(depth, mod_rows, N_ADA, d)

    xf = jnp.concatenate([x.reshape(n_lat, d), ctx.reshape(b * n_ctx, d)], axis=0)
    for l in range(depth):
        qt, k, vt, conf_in, sc_in = _inproj_call(
            xf, mods[l], g_norm_mix[l][None], w_in_b[l], gq_t[l], gk_t[l], cos_t, sin_t,
            b=b, s=s, attn_w=attn_w, kv_w=kv_w, conf_w2=2 * conf_w, sc_w3=3 * sconv_w)
        g_attn = g_grp[l][None, :attn_w]
        ya = _attn_call(qt, k, vt, g_attn, None, b=b, s=s, ctx=n_ctx, latent=True)
        ya = _attn_call(qt, k, vt, g_attn, ya, b=b, s=s, ctx=n_ctx, latent=False)
        yb = _conv_call(conf_in, sc_in, w_conf_dw[l], b_conf_dw[l][None], g_conf_ln[l][None], b_conf_ln[l][None],
                        w_sc_dw[l], g_grp[l][None, attn_w:attn_w + conf_w], g_grp[l][None, attn_w + conf_w:],
                        b=b, s=s, ctx=n_ctx)
        parts = []
        for tile0, part_tiles in part_ranges:
            xf, h2, route, counts = _outproj_call(xf, ya, yb, w_out_b[l], mods[l], g_norm_ffn[l][None], wr[l], br[l],
                                                  b=b, s=s, tile0=tile0, n_tiles=part_tiles)
            parts.append((h2, route, counts))
        gathered = []
        for h2, route, counts in parts:
            tok_buf, pos, blk_expert, n_used = _dispatch_plan(route, counts, n_experts)
            ys = _moe_call(blk_expert + l * n_experts, n_used, h2[tok_buf], w1, b1, w2, b2)
            gathered.append([ys[pos[:, kk]] for kk in range(TOP_K)])
        for (tile0, part_tiles), ys4, (h2, route, counts) in zip(part_ranges, gathered, parts):
            xf = _combine_call(xf, ys4, route, mods[l], b=b, s=s, tile0=tile0, n_tiles=part_tiles)
    return xf[:n_lat].reshape(b, s, d)
```

```python
import functools
import math

import jax
import jax.numpy as jnp
from jax import lax
from jax.experimental import pallas as pl
from jax.experimental.pallas import tpu as pltpu

GRID_W = 64
HEAD_DIM = 64
ROPE_THETA = 10000.0
TOP_K = 4
SWIGLU_ALPHA = 1.702
SWIGLU_LIMIT = 7.0
EPS = 1e-6
N_ADA = 6
LANES = 128
SUBLANES = 8
MXU_W = 256
NEG_BIG = -1e30
LOG2E = math.log2(math.e)

TOKEN_TILE = 512
Q_TILE = 512
KEY_CHUNK = 512
SCORE_COLS = 256
CONV_TILE = 256
CONF_HALO = 16
SC_HALO = 8
EXPERT_TILE = 512
MOE_PARTS = 3
WEIGHT_ROWS = 1024
ONES_ROWS = 16
VMEM_LIMIT = 48 * 1024 * 1024
ROUTE_IDX, ROUTE_PROB, ROUTE_RANK = 0, TOP_K, 2 * TOP_K

bf16 = jnp.bfloat16
f32 = jnp.float32


def _dot(a, b):
    return jnp.dot(a, b, preferred_element_type=f32)


def _sigmoid(x):
    return 1.0 / (1.0 + jnp.exp(-x))


def _split_bf16(x):
    hi = x.astype(bf16)
    lo = (x - hi.astype(f32)).astype(bf16)
    return hi, lo


def _params():
    return pltpu.CompilerParams(vmem_limit_bytes=VMEM_LIMIT)


def _ada_kernel(c_ref, w_ref, b_ref, o_ref):
    c = c_ref[...]
    s = c * _sigmoid(c)
    s_hi, s_lo = _split_bf16(s)
    w_hi, w_lo = _split_bf16(w_ref[0])
    o_ref[0] = _dot(s_hi, w_hi) + _dot(s_lo, w_hi) + _dot(s_hi, w_lo) + b_ref[0]


def _ada_call(cc, w_ada, b_ada):
    depth, d, n6 = w_ada.shape
    rows = cc.shape[0]
    tn = 1536
    return pl.pallas_call(
        _ada_kernel,
        grid=(depth, n6 // tn),
        in_specs=[pl.BlockSpec((rows, d), lambda l, j: (0, 0)),
                  pl.BlockSpec((1, d, tn), lambda l, j: (l, 0, j)),
                  pl.BlockSpec((1, 1, tn), lambda l, j: (l, 0, j))],
        out_specs=pl.BlockSpec((1, rows, tn), lambda l, j: (l, 0, j)),
        out_shape=jax.ShapeDtypeStruct((depth, rows, n6), f32),
        compiler_params=_params(),
    )(cc, w_ada, b_ada.reshape(depth, 1, n6))


def _w1_prep_kernel(w_ref, o_ref):
    r = lax.broadcasted_iota(jnp.int32, (MXU_W, MXU_W), 0)
    c = lax.broadcasted_iota(jnp.int32, (MXU_W, MXU_W), 1)
    src = jnp.where(c < LANES, 2 * c, 2 * (c - LANES) + 1)
    sel = jnp.where(r == src, 1.0, 0.0).astype(bf16)
    for grp in range(w_ref.shape[2] // MXU_W):
        cols = slice(grp * MXU_W, (grp + 1) * MXU_W)
        o_ref[0, :, cols] = _dot(w_ref[0, :, cols].astype(bf16), sel).astype(bf16)


def _cast_kernel(w_ref, o_ref):
    o_ref[...] = w_ref[...].astype(bf16)


def _weights_call(kern, w, rows):
    ne, d, n = w.shape
    return pl.pallas_call(
        kern,
        grid=(ne, d // rows),
        in_specs=[pl.BlockSpec((1, rows, n), lambda e, j: (e, j, 0))],
        out_specs=pl.BlockSpec((1, rows, n), lambda e, j: (e, j, 0)),
        out_shape=jax.ShapeDtypeStruct(w.shape, bf16),
        compiler_params=_params(),
    )(w)


def _modulated(x, g, mod, shift_row, scale_row):
    ms = jnp.mean(x * x, axis=-1, keepdims=True)
    xh = x * lax.rsqrt(ms + EPS)
    return xh * (g * (1.0 + mod[scale_row:scale_row + 1, :])) + mod[shift_row:shift_row + 1, :]


def _head_sumsq(y):
    n = y.shape[1]
    r = lax.broadcasted_iota(jnp.int32, (n, n), 0) // HEAD_DIM
    c = lax.broadcasted_iota(jnp.int32, (n, n), 1) // HEAD_DIM
    ones_bd = jnp.where(r == c, 1.0, 0.0).astype(bf16)
    return _dot((y * y).astype(bf16), ones_bd)


def _rope(y, cos, sin_signed):
    lane = lax.broadcasted_iota(jnp.int32, y.shape, 1)
    partner = jnp.where((lane % 32) < 16, pltpu.roll(y, LANES - 16, 1), pltpu.roll(y, 16, 1))
    return y * cos + partner * sin_signed


def _seg_index(i, tile, seg_len, n_seg):
    return jnp.minimum(i * tile // seg_len, n_seg)


def _inproj_kernel(x_ref, mod_ref, g_ref, w_ref, gq_ref, gk_ref, cos_ref, sin_ref,
                   qt_ref, k_ref, vt_ref, conf_ref, sc_ref, *, attn_w, kv_w, conf_w2):
    h = _modulated(x_ref[...], g_ref[...], mod_ref[0], 0, 1).astype(bf16)
    cos = cos_ref[...]
    sin = sin_ref[...]
    o_k = attn_w
    o_v = o_k + kv_w
    o_conf = o_v + kv_w
    o_sc = o_conf + conf_w2

    q = _dot(h, w_ref[:, 0:attn_w])
    for cb in range(attn_w // MXU_W):
        qb = q[:, cb * MXU_W:(cb + 1) * MXU_W]
        qn = qb * lax.rsqrt(_head_sumsq(qb) * (1.0 / HEAD_DIM) + EPS) * gq_ref[:, cb * MXU_W:(cb + 1) * MXU_W]
        for half in range(2):
            lo = cb * MXU_W + half * LANES
            y = _rope(qn[:, half * LANES:(half + 1) * LANES], cos, sin) * (LOG2E * HEAD_DIM ** -0.5)
            qt_ref[lo:lo + LANES, :] = y.T.astype(bf16)

    k = _dot(h, w_ref[:, o_k:o_v])
    kn = k * lax.rsqrt(_head_sumsq(k) * (1.0 / HEAD_DIM) + EPS) * gk_ref[...]
    k_ref[...] = _rope(kn, cos, sin).astype(bf16)

    vt = _dot(h, w_ref[:, o_v:o_conf]).T.astype(bf16)
    ones = jnp.ones((HEAD_DIM, vt.shape[1]), bf16)
    for hh in range(kv_w // HEAD_DIM):
        vt_ref[2 * hh * HEAD_DIM:(2 * hh + 1) * HEAD_DIM, :] = vt[hh * HEAD_DIM:(hh + 1) * HEAD_DIM, :]
        vt_ref[(2 * hh + 1) * HEAD_DIM:(2 * hh + 2) * HEAD_DIM, :] = ones

    conf_ref[...] = _dot(h, w_ref[:, o_conf:o_sc])
    sc_ref[...] = _dot(h, w_ref[:, o_sc:])


def _inproj_call(x, mods, g, w, gq, gk, cos_t, sin_t, *, b, s, attn_w, kv_w, conf_w2, sc_w3):
    n, d = x.shape
    tm = TOKEN_TILE
    n_lat_tiles = b * s // tm
    tiles_per_seq = s // tm
    in_w = w.shape[1]

    def tab_map(i):
        return (jnp.where(i < n_lat_tiles, i % tiles_per_seq, tiles_per_seq), 0)

    kern = functools.partial(_inproj_kernel, attn_w=attn_w, kv_w=kv_w, conf_w2=conf_w2)
    return pl.pallas_call(
        kern,
        grid=(n // tm,),
        in_specs=[pl.BlockSpec((tm, d), lambda i: (i, 0)),
                  pl.BlockSpec((1, N_ADA, d), lambda i: (_seg_index(i, tm, s, b), 0, 0)),
                  pl.BlockSpec((1, d), lambda i: (0, 0)),
                  pl.BlockSpec((d, in_w), lambda i: (0, 0)),
                  pl.BlockSpec((1, attn_w), lambda i: (0, 0)),
                  pl.BlockSpec((1, kv_w), lambda i: (0, 0)),
                  pl.BlockSpec((tm, LANES), tab_map),
                  pl.BlockSpec((tm, LANES), tab_map)],
        out_specs=[pl.BlockSpec((attn_w, tm), lambda i: (0, i)),
                   pl.BlockSpec((tm, kv_w), lambda i: (i, 0)),
                   pl.BlockSpec((2 * kv_w, tm), lambda i: (0, i)),
                   pl.BlockSpec((tm, conf_w2), lambda i: (i, 0)),
                   pl.BlockSpec((tm, sc_w3), lambda i: (i, 0))],
        out_shape=[jax.ShapeDtypeStruct((attn_w, n), bf16),
                   jax.ShapeDtypeStruct((n, kv_w), bf16),
                   jax.ShapeDtypeStruct((2 * kv_w, n), bf16),
                   jax.ShapeDtypeStruct((n, conf_w2), f32),
                   jax.ShapeDtypeStruct((n, sc_w3), f32)],
        compiler_params=_params(),
    )(x, mods, g, w, gq, gk, cos_t, sin_t)


def _attn_kernel(*refs, n_chunks1, chunk1, n_pairs, aliased):
    if aliased:
        refs = refs[1:]
    qt_ref, k1_ref, vt1_ref, k2_ref, vt2_ref, g_ref, o_ref, qs_ref, m_ref, acc_ref, st_ref, mx_ref = refs
    tq = qt_ref.shape[1]
    rows2 = k2_ref.shape[0]
    v_rows = HEAD_DIM + ONES_ROWS
    for grp in range(2):
        for j in range(n_pairs):
            head = grp * n_pairs + j
            qh = qt_ref[head * HEAD_DIM:(head + 1) * HEAD_DIM, :]
            cols = slice(j * tq, (j + 1) * tq)
            qs_ref[grp, grp * HEAD_DIM:(grp + 1) * HEAD_DIM, cols] = qh
            qs_ref[grp, (1 - grp) * HEAD_DIM:(2 - grp) * HEAD_DIM, cols] = jnp.zeros_like(qh)
    m_ref[...] = jnp.full(m_ref.shape, NEG_BIG, f32)
    acc_ref[...] = jnp.zeros(acc_ref.shape, f32)
    n_sub = n_pairs * tq // SCORE_COLS

    def produce(buf, kblk, grp, sb):
        cols = slice(sb * SCORE_COLS, (sb + 1) * SCORE_COLS)
        st = _dot(kblk, qs_ref[grp, :, cols])
        st_ref[buf, grp, 0:kblk.shape[0], cols] = st
        mx_ref[buf, grp, :, cols] = jnp.broadcast_to(jnp.max(st, axis=0, keepdims=True), (8, SCORE_COLS))

    def consume(buf, rows, vt_of, grp, sb):
        cols = slice(sb * SCORE_COLS, (sb + 1) * SCORE_COLS)
        st = st_ref[buf, grp, 0:rows, cols]
        m_old = m_ref[grp, :, cols]
        m_new = jnp.maximum(m_old, mx_ref[buf, grp, :, cols])
        alpha = jnp.exp2(m_old - m_new)
        pt = jnp.exp2(st - m_new[0:1, :]).astype(bf16)
        acc_ref[grp, :, cols] = acc_ref[grp, :, cols] * alpha[0:1, :] + _dot(vt_of(grp), pt)
        m_ref[grp, :, cols] = m_new

    def stage(buf_next, k_next, buf_cur, rows, vt_of):
        for grp in range(2):
            for sb in range(n_sub):
                if k_next is not None:
                    produce(buf_next, k_next, grp, sb)
                if vt_of is not None:
                    consume(buf_cur, rows, vt_of, grp, sb)

    def k1_at(c):
        return k1_ref[pl.ds(pl.multiple_of(c * chunk1, chunk1), chunk1), :]

    def vt1_at(c):
        start = pl.multiple_of(c * chunk1, chunk1)
        return lambda grp: vt1_ref[grp * LANES:grp * LANES + v_rows, pl.ds(start, chunk1)]

    vt2_of = lambda grp: vt2_ref[grp * LANES:grp * LANES + v_rows, :]

    if n_chunks1 > 0:
        assert n_chunks1 % 2 == 0
        stage(0, k1_at(0), None, None, None)

        def body(i, carry):
            stage(1, k1_at(2 * i + 1), 0, chunk1, vt1_at(2 * i))
            stage(0, k1_at(2 * i + 2), 1, chunk1, vt1_at(2 * i + 1))
            return carry
        lax.fori_loop(0, n_chunks1 // 2 - 1, body, 0)
        stage(1, k1_at(n_chunks1 - 1), 0, chunk1, vt1_at(n_chunks1 - 2))
        stage(0, k2_ref[...], 1, chunk1, vt1_at(n_chunks1 - 1))
    else:
        stage(0, k2_ref[...], None, None, None)
    stage(None, None, 0, rows2, vt2_of)

    outs = []
    for grp in range(2):
        acc = acc_ref[grp]
        outs.append(acc[0:HEAD_DIM, :] / acc[HEAD_DIM:HEAD_DIM + 1, :])
    ot = jnp.concatenate([outs[grp][:, j * tq:(j + 1) * tq] for grp in range(2) for j in range(n_pairs)], axis=0)
    inv = lax.rsqrt(jnp.mean(ot * ot, axis=0, keepdims=True) + EPS)
    o_ref[...] = ((ot * inv).T * g_ref[...]).astype(bf16)


def _attn_call(qt, k, vt, g, prev, *, b, s, ctx, latent):
    attn_w, n = qt.shape
    n_pairs = attn_w // LANES
    kv_w = k.shape[1]
    ctx_blk0 = b * s // ctx
    if latent:
        tq = Q_TILE
        per_b = s // tq
        grid = (b, per_b)
        q_idx = lambda bi, j: bi * per_b + j
        kv1_rows, n_chunks1, chunk1 = s, s // KEY_CHUNK, KEY_CHUNK
        kv1_idx = lambda bi, j: bi
    else:
        tq = ctx
        grid = (b, 1)
        q_idx = lambda bi, j: ctx_blk0 + bi
        kv1_rows, n_chunks1, chunk1 = ctx, 0, ctx
        kv1_idx = q_idx
    kv2_idx = lambda bi, j: ctx_blk0 + bi
    aliased = prev is not None
    kern = functools.partial(_attn_kernel, n_chunks1=n_chunks1, chunk1=chunk1, n_pairs=n_pairs, aliased=aliased)
    in_specs = [pl.BlockSpec((attn_w, tq), lambda bi, j: (0, q_idx(bi, j))),
                pl.BlockSpec((kv1_rows, kv_w), lambda bi, j: (kv1_idx(bi, j), 0)),
                pl.BlockSpec((2 * kv_w, kv1_rows), lambda bi, j: (0, kv1_idx(bi, j))),
                pl.BlockSpec((ctx, kv_w), lambda bi, j: (kv2_idx(bi, j), 0)),
                pl.BlockSpec((2 * kv_w, ctx), lambda bi, j: (0, kv2_idx(bi, j))),
                pl.BlockSpec((1, attn_w), lambda bi, j: (0, 0))]
    args = [qt, k, vt, k, vt, g]
    if aliased:
        in_specs = [pl.BlockSpec(memory_space=pl.ANY)] + in_specs
        args = [prev] + args
    return pl.pallas_call(
        kern,
        grid=grid,
        in_specs=in_specs,
        out_specs=pl.BlockSpec((tq, attn_w), lambda bi, j: (q_idx(bi, j), 0)),
        out_shape=jax.ShapeDtypeStruct((n, attn_w), bf16),
        scratch_shapes=[pltpu.VMEM((2, LANES, n_pairs * tq), bf16),
                        pltpu.VMEM((2, 8, n_pairs * tq), f32),
                        pltpu.VMEM((2, HEAD_DIM + ONES_ROWS, n_pairs * tq), f32),
                        pltpu.VMEM((2, 2, max(chunk1, ctx), n_pairs * tq), f32),
                        pltpu.VMEM((2, 2, 8, n_pairs * tq), f32)],
        input_output_aliases={0: 0} if aliased else {},
        compiler_params=_params(),
    )(*args)


def _conv_kernel(cp_ref, cc_ref, cn_ref, sp_ref, scc_ref, sn_ref, w31_ref, b31_ref, gln_ref, bln_ref,
                 w3_ref, gc_ref, gs_ref, o_ref, ext_ref, ext3_ref, shift_ref,
                 *, n_lat_tiles, tiles_per_seq, cw, taps31, taps3):
    i = pl.program_id(0)
    tc = cc_ref.shape[0]
    is_lat = i < n_lat_tiles
    left_edge = jnp.where(is_lat, (i % tiles_per_seq) == 0, True)
    right_edge = jnp.where(is_lat, (i % tiles_per_seq) == tiles_per_seq - 1, True)

    def glu(blk):
        return blk[:, 0:cw] * _sigmoid(blk[:, cw:2 * cw])

    ext_ref[0:CONF_HALO, :] = jnp.where(left_edge, 0.0, glu(cp_ref[...]))
    ext_ref[CONF_HALO:CONF_HALO + tc, :] = glu(cc_ref[...])
    ext_ref[CONF_HALO + tc:, :] = jnp.where(right_edge, 0.0, glu(cn_ref[...]))
    pad31 = (taps31 - 1) // 2
    u = jnp.zeros((tc, cw), f32) + b31_ref[...]
    offs = [CONF_HALO - pad31 + t for t in range(taps31)]
    for res in range(SUBLANES):
        taps = [t for t in range(taps31) if offs[t] % SUBLANES == res]
        if not taps:
            continue
        span = max(offs[t] for t in taps) - res + tc
        shift_ref[res, 0:span, :] = ext_ref[res:res + span, :]
        for t in taps:
            lo = offs[t] - res
            u = u + shift_ref[res, lo:lo + tc, :] * w31_ref[t:t + 1, :]
    mu = jnp.mean(u, axis=-1, keepdims=True)
    var = jnp.mean(jnp.square(u - mu), axis=-1, keepdims=True)
    y = (u - mu) * lax.rsqrt(var + EPS) * gln_ref[...] + bln_ref[...]
    conf = y * _sigmoid(y)

    def ch(blk):
        return blk[:, cw:2 * cw] * blk[:, 2 * cw:3 * cw]

    ext3_ref[0:SC_HALO, :] = jnp.where(left_edge, 0.0, ch(sp_ref[...]))
    ext3_ref[SC_HALO:SC_HALO + tc, :] = ch(scc_ref[...])
    ext3_ref[SC_HALO + tc:, :] = jnp.where(right_edge, 0.0, ch(sn_ref[...]))
    pad3 = (taps3 - 1) // 2
    acc = jnp.zeros((tc, cw), f32)
    for t in range(taps3):
        off = SC_HALO - pad3 + t
        acc = acc + ext3_ref[off:off + tc, :] * w3_ref[t:t + 1, :]
    sc = scc_ref[:, 0:cw] * acc

    def grp_norm(z, g_ref_):
        return z * lax.rsqrt(jnp.mean(z * z, axis=-1, keepdims=True) + EPS) * g_ref_[...]

    o_ref[:, 0:cw] = grp_norm(conf, gc_ref).astype(bf16)
    o_ref[:, cw:2 * cw] = grp_norm(sc, gs_ref).astype(bf16)


def _conv_call(conf_in, sc_in, w31, b31, gln, bln, w3, gc, gs, *, b, s, ctx):
    n = conf_in.shape[0]
    cw = w31.shape[1]
    tc = CONV_TILE
    assert ctx == tc and s % tc == 0
    n_tiles = n // tc
    ch_per = tc // CONF_HALO
    sh_per = tc // SC_HALO
    kern = functools.partial(_conv_kernel, n_lat_tiles=b * s // tc, tiles_per_seq=s // tc, cw=cw,
                             taps31=w31.shape[0], taps3=w3.shape[0])
    small = lambda a: pl.BlockSpec(a.shape, lambda i: (0, 0))
    return pl.pallas_call(
        kern,
        grid=(n_tiles,),
        in_specs=[pl.BlockSpec((CONF_HALO, 2 * cw), lambda i: (jnp.maximum(i * ch_per - 1, 0), 0)),
                  pl.BlockSpec((tc, 2 * cw), lambda i: (i, 0)),
                  pl.BlockSpec((CONF_HALO, 2 * cw), lambda i: (jnp.minimum((i + 1) * ch_per, n_tiles * ch_per - 1), 0)),
                  pl.BlockSpec((SC_HALO, 3 * cw), lambda i: (jnp.maximum(i * sh_per - 1, 0), 0)),
                  pl.BlockSpec((tc, 3 * cw), lambda i: (i, 0)),
                  pl.BlockSpec((SC_HALO, 3 * cw), lambda i: (jnp.minimum((i + 1) * sh_per, n_tiles * sh_per - 1), 0)),
                  small(w31), small(b31), small(gln), small(bln), small(w3), small(gc), small(gs)],
        out_specs=pl.BlockSpec((tc, 2 * cw), lambda i: (i, 0)),
        out_shape=jax.ShapeDtypeStruct((n, 2 * cw), bf16),
        scratch_shapes=[pltpu.VMEM((tc + 2 * CONF_HALO, cw), f32),
                        pltpu.VMEM((tc + 2 * SC_HALO, cw), f32),
                        pltpu.VMEM((SUBLANES, tc + 2 * CONF_HALO, cw), f32)],
        compiler_params=_params(),
    )(conf_in, conf_in, conf_in, sc_in, sc_in, sc_in, w31, b31, gln, bln, w3, gc, gs)


def _outproj_kernel(x_ref, hall_ref, ya_ref, yb_ref, w_ref, mod_ref, g_ref, wr_ref, br_ref,
                    xo_ref, h_ref, route_ref, cnt_ref, *, attn_w):
    @pl.when(pl.program_id(0) == 0)
    def _():
        cnt_ref[...] = jnp.zeros(cnt_ref.shape, f32)

    mod = mod_ref[0]
    y = _dot(ya_ref[...], w_ref[0:attn_w, :]) + _dot(yb_ref[...], w_ref[attn_w:, :])
    xn = x_ref[...] + mod[2:3, :] * y
    xo_ref[...] = xn
    h = _modulated(xn, g_ref[...], mod, 3, 4).astype(bf16)
    h_ref[...] = h

    logits = _dot(h, wr_ref[...]) + br_ref[...]
    tm = logits.shape[0]
    lane = lax.broadcasted_iota(jnp.int32, logits.shape, 1).astype(f32)
    vals, idxs = [], []
    chosen = jnp.zeros(logits.shape, f32)
    for _ in range(TOP_K):
        m = jnp.max(logits, axis=-1, keepdims=True)
        first = jnp.min(jnp.where(logits == m, lane, float(LANES)), axis=-1, keepdims=True)
        vals.append(m)
        idxs.append(first)
        hit = lane == first
        chosen = jnp.where(hit, 1.0, chosen)
        logits = jnp.where(hit, NEG_BIG, logits)
    es = [jnp.exp(vv - vals[0]) for vv in vals]
    den = es[0] + es[1] + es[2] + es[3]

    r = lax.broadcasted_iota(jnp.int32, (tm, tm), 0)
    c = lax.broadcasted_iota(jnp.int32, (tm, tm), 1)
    before = _dot(jnp.where(r > c, 1.0, 0.0).astype(bf16), chosen.astype(bf16)) + cnt_ref[0:1, :]
    cnt_ref[...] = cnt_ref[...] + jnp.sum(chosen, axis=0, keepdims=True)

    route = jnp.zeros(logits.shape, f32)
    for k in range(TOP_K):
        rank = jnp.sum(jnp.where(lane == idxs[k], before, 0.0), axis=-1, keepdims=True)
        route = jnp.where(lane == float(ROUTE_IDX + k), idxs[k], route)
        route = jnp.where(lane == float(ROUTE_PROB + k), es[k] / den, route)
        route = jnp.where(lane == float(ROUTE_RANK + k), rank, route)
    route_ref[...] = route


def _outproj_call(x, h_all, ya, yb, w, mods, g, wr, br, *, b, s, tile0, n_tiles):
    n, d = x.shape
    tm = TOKEN_TILE
    attn_w = ya.shape[1]
    kern = functools.partial(_outproj_kernel, attn_w=attn_w)
    row = lambda i: (i + tile0, 0)
    local = lambda i: (i, 0)
    fixed = lambda i: (0, 0)
    return pl.pallas_call(
        kern,
        grid=(n_tiles,),
        in_specs=[pl.BlockSpec((tm, d), row),
                  pl.BlockSpec(memory_space=pl.ANY),
                  pl.BlockSpec((tm, attn_w), row),
                  pl.BlockSpec((tm, yb.shape[1]), row),
                  pl.BlockSpec(w.shape, fixed),
                  pl.BlockSpec((1, N_ADA, d), lambda i: (_seg_index(i + tile0, tm, s, b), 0, 0)),
                  pl.BlockSpec((1, d), fixed),
                  pl.BlockSpec(wr.shape, fixed),
                  pl.BlockSpec((1, LANES), fixed)],
        out_specs=[pl.BlockSpec((tm, d), row), pl.BlockSpec((tm, d), row),
                   pl.BlockSpec((tm, LANES), local), pl.BlockSpec((8, LANES), fixed)],
        out_shape=[jax.ShapeDtypeStruct((n, d), f32), jax.ShapeDtypeStruct((n, d), bf16),
                   jax.ShapeDtypeStruct((n_tiles * tm, LANES), f32), jax.ShapeDtypeStruct((8, LANES), f32)],
        input_output_aliases={0: 0, 1: 1},
        compiler_params=_params(),
    )(x, h_all, ya, yb, w, mods, g, wr, br)


def _moe_kernel(be_ref, nu_ref, xs_ref, w1_ref, b1_ref, w2_ref, b2_ref, ys_ref):
    @pl.when(pl.program_id(0) < nu_ref[0])
    def _():
        u = _dot(xs_ref[...], w1_ref[0]) + b1_ref[0]
        acts = []
        for grp in range(u.shape[1] // MXU_W):
            gate = jnp.minimum(u[:, grp * MXU_W:grp * MXU_W + LANES], SWIGLU_LIMIT)
            up = jnp.clip(u[:, grp * MXU_W + LANES:(grp + 1) * MXU_W], -SWIGLU_LIMIT, SWIGLU_LIMIT)
            acts.append(((up + 1.0) * (gate * _sigmoid(SWIGLU_ALPHA * gate))).astype(bf16))
        ys_ref[...] = (_dot(jnp.concatenate(acts, axis=1), w2_ref[0]) + b2_ref[0]).astype(bf16)


def _moe_call(blk_expert, n_used, xs, w1, b1, w2, b2):
    cap, d = xs.shape
    bm = EXPERT_TILE
    de2 = w1.shape[2]
    row = lambda i, be, nu: (jnp.minimum(i, nu[0] - 1), 0)
    exp3 = lambda i, be, nu: (be[i], 0, 0)
    return pl.pallas_call(
        _moe_kernel,
        grid_spec=pltpu.PrefetchScalarGridSpec(
            num_scalar_prefetch=2,
            grid=(cap // bm,),
            in_specs=[pl.BlockSpec((bm, d), row),
                      pl.BlockSpec((1, d, de2), exp3),
                      pl.BlockSpec((1, 1, de2), exp3),
                      pl.BlockSpec((1, de2 // 2, d), exp3),
                      pl.BlockSpec((1, 1, d), exp3)],
            out_specs=pl.BlockSpec((bm, d), row)),
        out_shape=jax.ShapeDtypeStruct((cap, d), bf16),
        compiler_params=_params(),
    )(blk_expert, n_used, xs, w1, b1, w2, b2)


def _combine_kernel(x_ref, y0_ref, y1_ref, y2_ref, y3_ref, route_ref, mod_ref, o_ref):
    route = route_ref[...]
    f = jnp.zeros(x_ref.shape, f32)
    for k, y_ref in enumerate((y0_ref, y1_ref, y2_ref, y3_ref)):
        f = f + route[:, ROUTE_PROB + k:ROUTE_PROB + k + 1] * y_ref[...].astype(f32)
    o_ref[...] = x_ref[...] + mod_ref[0][5:6, :] * f


def _combine_call(x, ys4, route, mods, *, b, s, tile0, n_tiles):
    n, d = x.shape
    tm = TOKEN_TILE
    row = lambda i: (i + tile0, 0)
    local = lambda i: (i, 0)
    return pl.pallas_call(
        _combine_kernel,
        grid=(n_tiles,),
        in_specs=[pl.BlockSpec((tm, d), row)] + [pl.BlockSpec((tm, d), local)] * TOP_K +
                 [pl.BlockSpec((tm, LANES), local),
                  pl.BlockSpec((1, N_ADA, d), lambda i: (_seg_index(i + tile0, tm, s, b), 0, 0))],
        out_specs=pl.BlockSpec((tm, d), row),
        out_shape=jax.ShapeDtypeStruct((n, d), f32),
        input_output_aliases={0: 0},
        compiler_params=_params(),
    )(x, *ys4, route, mods)


def _dispatch_plan(route, counts_f, n_experts):
    n_tok = route.shape[0]
    n_pairs = n_tok * TOP_K
    bm = EXPERT_TILE
    pair_bits = (n_pairs - 1).bit_length()
    assert pair_bits + (n_experts - 1).bit_length() < 31
    top_i = route[:, ROUTE_IDX:ROUTE_IDX + TOP_K].astype(jnp.int32)
    rank = route[:, ROUTE_RANK:ROUTE_RANK + TOP_K].astype(jnp.int32)
    counts = counts_f[0, :n_experts].astype(jnp.int32)
    starts = jnp.cumsum(counts) - counts
    padded = (counts + bm - 1) // bm * bm
    pad_ends = jnp.cumsum(padded)
    pad_starts = pad_ends - padded
    experts = jnp.arange(n_experts, dtype=jnp.int32)
    pos = jnp.sum(jnp.where(top_i[:, :, None] == experts, pad_starts, 0), axis=-1) + rank
    n_blocks = -(-n_pairs // bm) + n_experts
    blk_start = jnp.arange(n_blocks, dtype=jnp.int32) * bm
    blk_expert = jnp.minimum(jnp.sum(blk_start[:, None] >= pad_ends[None, :], axis=-1), n_experts - 1).astype(jnp.int32)
    n_used = (pad_ends[-1] // bm).astype(jnp.int32).reshape(1)
    pair_id = jnp.arange(n_pairs, dtype=jnp.int32)
    keys = jnp.sort(top_i.reshape(-1) * (1 << pair_bits) + pair_id)
    sorted_tok = (keys & ((1 << pair_bits) - 1)) // TOP_K
    shift = (starts - pad_starts)[blk_expert]
    src = jnp.clip((blk_start + shift)[:, None] + jnp.arange(bm, dtype=jnp.int32)[None, :], 0, n_pairs - 1)
    tok_buf = sorted_tok[src.reshape(-1)]
    return tok_buf, pos, blk_expert, n_used


def _rope_tables(s, extra_rows):
    rows = s // GRID_W
    axis_dim = HEAD_DIM // 2
    half = axis_dim // 2
    row = jnp.repeat(jnp.arange(rows, dtype=f32), GRID_W)
    col = jnp.tile(jnp.arange(GRID_W, dtype=f32), rows)
    inv = ROPE_THETA ** (-jnp.arange(0, axis_dim, 2, dtype=f32) / axis_dim)
    ang = jnp.concatenate([row[:, None] * inv, col[:, None] * inv], axis=-1)
    lane = jnp.arange(LANES)
    src = ((lane % HEAD_DIM) // axis_dim) * half + lane % half
    sign = jnp.where((lane % axis_dim) < half, -1.0, 1.0).astype(f32)
    cos_t = jnp.cos(ang)[:, src]
    sin_t = jnp.sin(ang)[:, src] * sign
    cos_t = jnp.concatenate([cos_t, jnp.ones((extra_rows, LANES), f32)], axis=0)
    sin_t = jnp.concatenate([sin_t, jnp.zeros((extra_rows, LANES), f32)], axis=0)
    return cos_t, sin_t


def kernel(x, c, ctx, c_ctx, w_ada, b_ada, g_norm_mix, g_norm_ffn, w_in, g_q, g_k, w_conf_dw, b_conf_dw,
           g_conf_ln, b_conf_ln, w_sc_dw, g_grp, w_out, w_router, b_router, w_e_in, b_e_in, w_e_out, b_e_out):
    b, s, d = x.shape
    n_ctx = ctx.shape[1]
    depth = w_ada.shape[0]
    n_experts = w_router.shape[2]
    d_exp = w_e_out.shape[2]
    attn_w = d // 2
    n_q = attn_w // HEAD_DIM
    n_kv = max(n_q // 4, 1)
    kv_w = n_kv * HEAD_DIM
    conf_w = d // 4
    sconv_w = d - attn_w - conf_w
    assert n_kv == 2 and kv_w == LANES and conf_w == sconv_w
    n_lat = b * s
    n_tiles = (n_lat + b * n_ctx) // TOKEN_TILE
    part_sizes = [n_tiles // MOE_PARTS + (1 if p < n_tiles % MOE_PARTS else 0) for p in range(MOE_PARTS)]
    part_ranges = [(sum(part_sizes[:p]), part_sizes[p]) for p in range(MOE_PARTS)]
    assert s % TOKEN_TILE == 0 and (b * n_ctx) % TOKEN_TILE == 0 and s % KEY_CHUNK == 0 and s % Q_TILE == 0

    w_in_b = w_in.astype(bf16)
    w_out_b = w_out.astype(bf16)
    gq_t = jnp.tile(g_q, (1, n_q)).reshape(depth, 1, attn_w)
    gk_t = jnp.tile(g_k, (1, n_kv)).reshape(depth, 1, kv_w)
    cos_t, sin_t = _rope_tables(s, TOKEN_TILE)

    w1 = _weights_call(_w1_prep_kernel, w_e_in.reshape(depth * n_experts, d, 2 * d_exp), WEIGHT_ROWS)
    w2 = _weights_call(_cast_kernel, w_e_out.reshape(depth * n_experts, d_exp, d), WEIGHT_ROWS)
    b1 = b_e_in.reshape(depth * n_experts, 2 * d_exp // MXU_W, LANES, 2)
    b1 = jnp.swapaxes(b1, -1, -2).reshape(depth * n_experts, 1, 2 * d_exp)
    b2 = b_e_out.reshape(depth * n_experts, 1, d)
    wr = jnp.pad(w_router, ((0, 0), (0, 0), (0, LANES - n_experts))).astype(bf16)
    br = jnp.pad(b_router, ((0, 0), (0, LANES - n_experts)), constant_values=NEG_BIG).reshape(depth, 1, LANES)

    mod_rows = b + 1
    pad_rows = -mod_rows % 8
    cc = jnp.concatenate([c, c_ctx[None, :], jnp.zeros((pad_rows, d), f32)], axis=0)
    mods = _ada_call(cc, w_ada, b_ada)[:, :mod_rows].reshape(depth, mod_rows, N_ADA, d)

    xf = jnp.concatenate([x.reshape(n_lat, d), ctx.reshape(b * n_ctx, d)], axis=0)
    h2 = jnp.zeros(xf.shape, bf16)
    for l in range(depth):
        qt, k, vt, conf_in, sc_in = _inproj_call(
            xf, mods[l], g_norm_mix[l][None], w_in_b[l], gq_t[l], gk_t[l], cos_t, sin_t,
            b=b, s=s, attn_w=attn_w, kv_w=kv_w, conf_w2=2 * conf_w, sc_w3=3 * sconv_w)
        g_attn = g_grp[l][None, :attn_w]
        ya = _attn_call(qt, k, vt, g_attn, None, b=b, s=s, ctx=n_ctx, latent=True)
        ya = _attn_call(qt, k, vt, g_attn, ya, b=b, s=s, ctx=n_ctx, latent=False)
        yb = _conv_call(conf_in, sc_in, w_conf_dw[l], b_conf_dw[l][None], g_conf_ln[l][None], b_conf_ln[l][None],
                        w_sc_dw[l], g_grp[l][None, attn_w:attn_w + conf_w], g_grp[l][None, attn_w + conf_w:],
                        b=b, s=s, ctx=n_ctx)
        parts = []
        for tile0, part_tiles in part_ranges:
            xf, h2, route, counts = _outproj_call(xf, h2, ya, yb, w_out_b[l], mods[l], g_norm_ffn[l][None],
                                                  wr[l], br[l], b=b, s=s, tile0=tile0, n_tiles=part_tiles)
            parts.append((route, counts))
        gathered = []
        for (tile0, part_tiles), (route, counts) in zip(part_ranges, parts):
            tok_buf, pos, blk_expert, n_used = _dispatch_plan(route, counts, n_experts)
            xs = h2[tok_buf + tile0 * TOKEN_TILE]
            ys = _moe_call(blk_expert + l * n_experts, n_used, xs, w1, b1, w2, b2)
            gathered.append([ys[pos[:, kk]] for kk in range(TOP_K)])
        for (tile0, part_tiles), ys4, (route, counts) in zip(part_ranges, gathered, parts):
            xf = _combine_call(xf, ys4, route, mods[l], b=b, s=s, tile0=tile0, n_tiles=part_tiles)
    return xf[:n_lat].reshape(b, s, d)
```

```python
import functools
import math

import jax
import jax.numpy as jnp
from jax import lax
from jax.experimental import pallas as pl
from jax.experimental.pallas import tpu as pltpu

GRID_W = 64
HEAD_DIM = 64
ROPE_THETA = 10000.0
TOP_K = 4
SWIGLU_ALPHA = 1.702
SWIGLU_LIMIT = 7.0
EPS = 1e-6
N_ADA = 6
LANES = 128
SUBLANES = 8
MXU_W = 256
NEG_BIG = -1e30
LOG2E = math.log2(math.e)

TOKEN_TILE = 512
Q_TILE = 512
KEY_CHUNK = 512
SCORE_COLS = 256
CONV_TILE = 256
CONF_HALO = 16
SC_HALO = 8
EXPERT_TILE = 512
MOE_PARTS = 2
WEIGHT_ROWS = 1024
ONES_ROWS = 16
VMEM_LIMIT = 48 * 1024 * 1024
ROUTE_IDX, ROUTE_PROB, ROUTE_RANK = 0, TOP_K, 2 * TOP_K

bf16 = jnp.bfloat16
f32 = jnp.float32


def _dot(a, b):
    return jnp.dot(a, b, preferred_element_type=f32)


def _sigmoid(x):
    return 1.0 / (1.0 + jnp.exp(-x))


def _split_bf16(x):
    hi = x.astype(bf16)
    lo = (x - hi.astype(f32)).astype(bf16)
    return hi, lo


def _params():
    return pltpu.CompilerParams(vmem_limit_bytes=VMEM_LIMIT)


def _ada_kernel(c_ref, w_ref, b_ref, o_ref):
    c = c_ref[...]
    s = c * _sigmoid(c)
    s_hi, s_lo = _split_bf16(s)
    w_hi, w_lo = _split_bf16(w_ref[0])
    o_ref[0] = _dot(s_hi, w_hi) + _dot(s_lo, w_hi) + _dot(s_hi, w_lo) + b_ref[0]


def _ada_call(cc, w_ada, b_ada):
    depth, d, n6 = w_ada.shape
    rows = cc.shape[0]
    tn = 1536
    return pl.pallas_call(
        _ada_kernel,
        grid=(depth, n6 // tn),
        in_specs=[pl.BlockSpec((rows, d), lambda l, j: (0, 0)),
                  pl.BlockSpec((1, d, tn), lambda l, j: (l, 0, j)),
                  pl.BlockSpec((1, 1, tn), lambda l, j: (l, 0, j))],
        out_specs=pl.BlockSpec((1, rows, tn), lambda l, j: (l, 0, j)),
        out_shape=jax.ShapeDtypeStruct((depth, rows, n6), f32),
        compiler_params=_params(),
    )(cc, w_ada, b_ada.reshape(depth, 1, n6))


def _w1_prep_kernel(w_ref, o_ref):
    r = lax.broadcasted_iota(jnp.int32, (MXU_W, MXU_W), 0)
    c = lax.broadcasted_iota(jnp.int32, (MXU_W, MXU_W), 1)
    src = jnp.where(c < LANES, 2 * c, 2 * (c - LANES) + 1)
    sel = jnp.where(r == src, 1.0, 0.0).astype(bf16)
    for grp in range(w_ref.shape[2] // MXU_W):
        cols = slice(grp * MXU_W, (grp + 1) * MXU_W)
        o_ref[0, :, cols] = _dot(w_ref[0, :, cols].astype(bf16), sel).astype(bf16)


def _cast_kernel(w_ref, o_ref):
    o_ref[...] = w_ref[...].astype(bf16)


def _weights_call(kern, w, rows):
    ne, d, n = w.shape
    return pl.pallas_call(
        kern,
        grid=(ne, d // rows),
        in_specs=[pl.BlockSpec((1, rows, n), lambda e, j: (e, j, 0))],
        out_specs=pl.BlockSpec((1, rows, n), lambda e, j: (e, j, 0)),
        out_shape=jax.ShapeDtypeStruct(w.shape, bf16),
        compiler_params=_params(),
    )(w)


def _modulated(x, g, mod, shift_row, scale_row):
    ms = jnp.mean(x * x, axis=-1, keepdims=True)
    xh = x * lax.rsqrt(ms + EPS)
    return xh * (g * (1.0 + mod[scale_row:scale_row + 1, :])) + mod[shift_row:shift_row + 1, :]


def _head_sumsq(y):
    n = y.shape[1]
    r = lax.broadcasted_iota(jnp.int32, (n, n), 0) // HEAD_DIM
    c = lax.broadcasted_iota(jnp.int32, (n, n), 1) // HEAD_DIM
    ones_bd = jnp.where(r == c, 1.0, 0.0).astype(bf16)
    return _dot((y * y).astype(bf16), ones_bd)


def _rope(y, cos, sin_signed):
    lane = lax.broadcasted_iota(jnp.int32, y.shape, 1)
    partner = jnp.where((lane % 32) < 16, pltpu.roll(y, LANES - 16, 1), pltpu.roll(y, 16, 1))
    return y * cos + partner * sin_signed


def _seg_index(i, tile, seg_len, n_seg):
    return jnp.minimum(i * tile // seg_len, n_seg)


def _inproj_kernel(x_ref, mod_ref, g_ref, w_ref, gq_ref, gk_ref, cos_ref, sin_ref,
                   qt_ref, k_ref, vt_ref, conf_ref, sc_ref, *, attn_w, kv_w, conf_w2):
    h = _modulated(x_ref[...], g_ref[...], mod_ref[0], 0, 1).astype(bf16)
    cos = cos_ref[...]
    sin = sin_ref[...]
    o_k = attn_w
    o_v = o_k + kv_w
    o_conf = o_v + kv_w
    o_sc = o_conf + conf_w2

    q = _dot(h, w_ref[:, 0:attn_w])
    for cb in range(attn_w // MXU_W):
        qb = q[:, cb * MXU_W:(cb + 1) * MXU_W]
        qn = qb * lax.rsqrt(_head_sumsq(qb) * (1.0 / HEAD_DIM) + EPS) * gq_ref[:, cb * MXU_W:(cb + 1) * MXU_W]
        for half in range(2):
            lo = cb * MXU_W + half * LANES
            y = _rope(qn[:, half * LANES:(half + 1) * LANES], cos, sin) * (LOG2E * HEAD_DIM ** -0.5)
            qt_ref[lo:lo + LANES, :] = y.T.astype(bf16)

    k = _dot(h, w_ref[:, o_k:o_v])
    kn = k * lax.rsqrt(_head_sumsq(k) * (1.0 / HEAD_DIM) + EPS) * gk_ref[...]
    k_ref[...] = _rope(kn, cos, sin).astype(bf16)

    vt = _dot(h, w_ref[:, o_v:o_conf]).T.astype(bf16)
    ones = jnp.ones((HEAD_DIM, vt.shape[1]), bf16)
    for hh in range(kv_w // HEAD_DIM):
        vt_ref[2 * hh * HEAD_DIM:(2 * hh + 1) * HEAD_DIM, :] = vt[hh * HEAD_DIM:(hh + 1) * HEAD_DIM, :]
        vt_ref[(2 * hh + 1) * HEAD_DIM:(2 * hh + 2) * HEAD_DIM, :] = ones

    conf_ref[...] = _dot(h, w_ref[:, o_conf:o_sc])
    sc_ref[...] = _dot(h, w_ref[:, o_sc:])


def _inproj_call(x, mods, g, w, gq, gk, cos_t, sin_t, *, b, s, attn_w, kv_w, conf_w2, sc_w3):
    n, d = x.shape
    tm = TOKEN_TILE
    n_lat_tiles = b * s // tm
    tiles_per_seq = s // tm
    in_w = w.shape[1]

    def tab_map(i):
        return (jnp.where(i < n_lat_tiles, i % tiles_per_seq, tiles_per_seq), 0)

    kern = functools.partial(_inproj_kernel, attn_w=attn_w, kv_w=kv_w, conf_w2=conf_w2)
    return pl.pallas_call(
        kern,
        grid=(n // tm,),
        in_specs=[pl.BlockSpec((tm, d), lambda i: (i, 0)),
                  pl.BlockSpec((1, N_ADA, d), lambda i: (_seg_index(i, tm, s, b), 0, 0)),
                  pl.BlockSpec((1, d), lambda i: (0, 0)),
                  pl.BlockSpec((d, in_w), lambda i: (0, 0)),
                  pl.BlockSpec((1, attn_w), lambda i: (0, 0)),
                  pl.BlockSpec((1, kv_w), lambda i: (0, 0)),
                  pl.BlockSpec((tm, LANES), tab_map),
                  pl.BlockSpec((tm, LANES), tab_map)],
        out_specs=[pl.BlockSpec((attn_w, tm), lambda i: (0, i)),
                   pl.BlockSpec((tm, kv_w), lambda i: (i, 0)),
                   pl.BlockSpec((2 * kv_w, tm), lambda i: (0, i)),
                   pl.BlockSpec((tm, conf_w2), lambda i: (i, 0)),
                   pl.BlockSpec((tm, sc_w3), lambda i: (i, 0))],
        out_shape=[jax.ShapeDtypeStruct((attn_w, n), bf16),
                   jax.ShapeDtypeStruct((n, kv_w), bf16),
                   jax.ShapeDtypeStruct((2 * kv_w, n), bf16),
                   jax.ShapeDtypeStruct((n, conf_w2), f32),
                   jax.ShapeDtypeStruct((n, sc_w3), f32)],
        compiler_params=_params(),
    )(x, mods, g, w, gq, gk, cos_t, sin_t)


def _attn_kernel(*refs, n_chunks1, chunk1, n_pairs, aliased):
    if aliased:
        refs = refs[1:]
    qt_ref, k1_ref, vt1_ref, k2_ref, vt2_ref, g_ref, o_ref, qs_ref, m_ref, acc_ref, st_ref, mx_ref = refs
    tq = qt_ref.shape[1]
    rows2 = k2_ref.shape[0]
    v_rows = HEAD_DIM + ONES_ROWS
    for grp in range(2):
        for j in range(n_pairs):
            head = grp * n_pairs + j
            qh = qt_ref[head * HEAD_DIM:(head + 1) * HEAD_DIM, :]
            cols = slice(j * tq, (j + 1) * tq)
            qs_ref[grp, grp * HEAD_DIM:(grp + 1) * HEAD_DIM, cols] = qh
            qs_ref[grp, (1 - grp) * HEAD_DIM:(2 - grp) * HEAD_DIM, cols] = jnp.zeros_like(qh)
    m_ref[...] = jnp.full(m_ref.shape, NEG_BIG, f32)
    acc_ref[...] = jnp.zeros(acc_ref.shape, f32)
    n_sub = n_pairs * tq // SCORE_COLS

    def produce(buf, kblk, grp, sb):
        cols = slice(sb * SCORE_COLS, (sb + 1) * SCORE_COLS)
        st = _dot(kblk, qs_ref[grp, :, cols])
        st_ref[buf, grp, 0:kblk.shape[0], cols] = st
        mx_ref[buf, grp, :, cols] = jnp.broadcast_to(jnp.max(st, axis=0, keepdims=True), (8, SCORE_COLS))

    def consume(buf, rows, vt_of, grp, sb):
        cols = slice(sb * SCORE_COLS, (sb + 1) * SCORE_COLS)
        st = st_ref[buf, grp, 0:rows, cols]
        m_old = m_ref[grp, :, cols]
        m_new = jnp.maximum(m_old, mx_ref[buf, grp, :, cols])
        alpha = jnp.exp2(m_old - m_new)
        pt = jnp.exp2(st - m_new[0:1, :]).astype(bf16)
        acc_ref[grp, :, cols] = acc_ref[grp, :, cols] * alpha[0:1, :] + _dot(vt_of(grp), pt)
        m_ref[grp, :, cols] = m_new

    def stage(buf_next, k_next, buf_cur, rows, vt_of):
        for grp in range(2):
            for sb in range(n_sub):
                if k_next is not None:
                    produce(buf_next, k_next, grp, sb)
                if vt_of is not None:
                    consume(buf_cur, rows, vt_of, grp, sb)

    def k1_at(c):
        return k1_ref[pl.ds(pl.multiple_of(c * chunk1, chunk1), chunk1), :]

    def vt1_at(c):
        start = pl.multiple_of(c * chunk1, chunk1)
        return lambda grp: vt1_ref[grp * LANES:grp * LANES + v_rows, pl.ds(start, chunk1)]

    vt2_of = lambda grp: vt2_ref[grp * LANES:grp * LANES + v_rows, :]

    if n_chunks1 > 0:
        assert n_chunks1 % 2 == 0
        stage(0, k1_at(0), None, None, None)

        def body(i, carry):
            stage(1, k1_at(2 * i + 1), 0, chunk1, vt1_at(2 * i))
            stage(0, k1_at(2 * i + 2), 1, chunk1, vt1_at(2 * i + 1))
            return carry
        lax.fori_loop(0, n_chunks1 // 2 - 1, body, 0)
        stage(1, k1_at(n_chunks1 - 1), 0, chunk1, vt1_at(n_chunks1 - 2))
        stage(0, k2_ref[...], 1, chunk1, vt1_at(n_chunks1 - 1))
    else:
        stage(0, k2_ref[...], None, None, None)
    stage(None, None, 0, rows2, vt2_of)

    outs = []
    for grp in range(2):
        acc = acc_ref[grp]
        outs.append(acc[0:HEAD_DIM, :] / acc[HEAD_DIM:HEAD_DIM + 1, :])
    ot = jnp.concatenate([outs[grp][:, j * tq:(j + 1) * tq] for grp in range(2) for j in range(n_pairs)], axis=0)
    inv = lax.rsqrt(jnp.mean(ot * ot, axis=0, keepdims=True) + EPS)
    o_ref[...] = ((ot * inv).T * g_ref[...]).astype(bf16)


def _attn_call(qt, k, vt, g, prev, *, b, s, ctx, latent):
    attn_w, n = qt.shape
    n_pairs = attn_w // LANES
    kv_w = k.shape[1]
    ctx_blk0 = b * s // ctx
    if latent:
        tq = Q_TILE
        per_b = s // tq
        grid = (b, per_b)
        q_idx = lambda bi, j: bi * per_b + j
        kv1_rows, n_chunks1, chunk1 = s, s // KEY_CHUNK, KEY_CHUNK
        kv1_idx = lambda bi, j: bi
    else:
        tq = ctx
        grid = (b, 1)
        q_idx = lambda bi, j: ctx_blk0 + bi
        kv1_rows, n_chunks1, chunk1 = ctx, 0, ctx
        kv1_idx = q_idx
    kv2_idx = lambda bi, j: ctx_blk0 + bi
    aliased = prev is not None
    kern = functools.partial(_attn_kernel, n_chunks1=n_chunks1, chunk1=chunk1, n_pairs=n_pairs, aliased=aliased)
    in_specs = [pl.BlockSpec((attn_w, tq), lambda bi, j: (0, q_idx(bi, j))),
                pl.BlockSpec((kv1_rows, kv_w), lambda bi, j: (kv1_idx(bi, j), 0)),
                pl.BlockSpec((2 * kv_w, kv1_rows), lambda bi, j: (0, kv1_idx(bi, j))),
                pl.BlockSpec((ctx, kv_w), lambda bi, j: (kv2_idx(bi, j), 0)),
                pl.BlockSpec((2 * kv_w, ctx), lambda bi, j: (0, kv2_idx(bi, j))),
                pl.BlockSpec((1, attn_w), lambda bi, j: (0, 0))]
    args = [qt, k, vt, k, vt, g]
    if aliased:
        in_specs = [pl.BlockSpec(memory_space=pl.ANY)] + in_specs
        args = [prev] + args
    return pl.pallas_call(
        kern,
        grid=grid,
        in_specs=in_specs,
        out_specs=pl.BlockSpec((tq, attn_w), lambda bi, j: (q_idx(bi, j), 0)),
        out_shape=jax.ShapeDtypeStruct((n, attn_w), bf16),
        scratch_shapes=[pltpu.VMEM((2, LANES, n_pairs * tq), bf16),
                        pltpu.VMEM((2, 8, n_pairs * tq), f32),
                        pltpu.VMEM((2, HEAD_DIM + ONES_ROWS, n_pairs * tq), f32),
                        pltpu.VMEM((2, 2, max(chunk1, ctx), n_pairs * tq), f32),
                        pltpu.VMEM((2, 2, 8, n_pairs * tq), f32)],
        input_output_aliases={0: 0} if aliased else {},
        compiler_params=_params(),
    )(*args)


def _conv_kernel(cp_ref, cc_ref, cn_ref, sp_ref, scc_ref, sn_ref, w31_ref, b31_ref, gln_ref, bln_ref,
                 w3_ref, gc_ref, gs_ref, o_ref, ext_ref, ext3_ref, shift_ref,
                 *, n_lat_tiles, tiles_per_seq, cw, taps31, taps3):
    i = pl.program_id(0)
    tc = cc_ref.shape[0]
    is_lat = i < n_lat_tiles
    left_edge = jnp.where(is_lat, (i % tiles_per_seq) == 0, True)
    right_edge = jnp.where(is_lat, (i % tiles_per_seq) == tiles_per_seq - 1, True)

    def glu(blk):
        return blk[:, 0:cw] * _sigmoid(blk[:, cw:2 * cw])

    ext_ref[0:CONF_HALO, :] = jnp.where(left_edge, 0.0, glu(cp_ref[...]))
    ext_ref[CONF_HALO:CONF_HALO + tc, :] = glu(cc_ref[...])
    ext_ref[CONF_HALO + tc:, :] = jnp.where(right_edge, 0.0, glu(cn_ref[...]))
    pad31 = (taps31 - 1) // 2
    u = jnp.zeros((tc, cw), f32) + b31_ref[...]
    offs = [CONF_HALO - pad31 + t for t in range(taps31)]
    for res in range(SUBLANES):
        taps = [t for t in range(taps31) if offs[t] % SUBLANES == res]
        if not taps:
            continue
        span = max(offs[t] for t in taps) - res + tc
        shift_ref[res, 0:span, :] = ext_ref[res:res + span, :]
        for t in taps:
            lo = offs[t] - res
            u = u + shift_ref[res, lo:lo + tc, :] * w31_ref[t:t + 1, :]
    mu = jnp.mean(u, axis=-1, keepdims=True)
    var = jnp.mean(jnp.square(u - mu), axis=-1, keepdims=True)
    y = (u - mu) * lax.rsqrt(var + EPS) * gln_ref[...] + bln_ref[...]
    conf = y * _sigmoid(y)

    def ch(blk):
        return blk[:, cw:2 * cw] * blk[:, 2 * cw:3 * cw]

    ext3_ref[0:SC_HALO, :] = jnp.where(left_edge, 0.0, ch(sp_ref[...]))
    ext3_ref[SC_HALO:SC_HALO + tc, :] = ch(scc_ref[...])
    ext3_ref[SC_HALO + tc:, :] = jnp.where(right_edge, 0.0, ch(sn_ref[...]))
    pad3 = (taps3 - 1) // 2
    acc = jnp.zeros((tc, cw), f32)
    for t in range(taps3):
        off = SC_HALO - pad3 + t
        acc = acc + ext3_ref[off:off + tc, :] * w3_ref[t:t + 1, :]
    sc = scc_ref[:, 0:cw] * acc

    def grp_norm(z, g_ref_):
        return z * lax.rsqrt(jnp.mean(z * z, axis=-1, keepdims=True) + EPS) * g_ref_[...]

    o_ref[:, 0:cw] = grp_norm(conf, gc_ref).astype(bf16)
    o_ref[:, cw:2 * cw] = grp_norm(sc, gs_ref).astype(bf16)


def _conv_call(conf_in, sc_in, w31, b31, gln, bln, w3, gc, gs, *, b, s, ctx):
    n = conf_in.shape[0]
    cw = w31.shape[1]
    tc = CONV_TILE
    assert ctx == tc and s % tc == 0
    n_tiles = n // tc
    ch_per = tc // CONF_HALO
    sh_per = tc // SC_HALO
    kern = functools.partial(_conv_kernel, n_lat_tiles=b * s // tc, tiles_per_seq=s // tc, cw=cw,
                             taps31=w31.shape[0], taps3=w3.shape[0])
    small = lambda a: pl.BlockSpec(a.shape, lambda i: (0, 0))
    return pl.pallas_call(
        kern,
        grid=(n_tiles,),
        in_specs=[pl.BlockSpec((CONF_HALO, 2 * cw), lambda i: (jnp.maximum(i * ch_per - 1, 0), 0)),
                  pl.BlockSpec((tc, 2 * cw), lambda i: (i, 0)),
                  pl.BlockSpec((CONF_HALO, 2 * cw), lambda i: (jnp.minimum((i + 1) * ch_per, n_tiles * ch_per - 1), 0)),
                  pl.BlockSpec((SC_HALO, 3 * cw), lambda i: (jnp.maximum(i * sh_per - 1, 0), 0)),
                  pl.BlockSpec((tc, 3 * cw), lambda i: (i, 0)),
                  pl.BlockSpec((SC_HALO, 3 * cw), lambda i: (jnp.minimum((i + 1) * sh_per, n_tiles * sh_per - 1), 0)),
                  small(w31), small(b31), small(gln), small(bln), small(w3), small(gc), small(gs)],
        out_specs=pl.BlockSpec((tc, 2 * cw), lambda i: (i, 0)),
        out_shape=jax.ShapeDtypeStruct((n, 2 * cw), bf16),
        scratch_shapes=[pltpu.VMEM((tc + 2 * CONF_HALO, cw), f32),
                        pltpu.VMEM((tc + 2 * SC_HALO, cw), f32),
                        pltpu.VMEM((SUBLANES, tc + 2 * CONF_HALO, cw), f32)],
        compiler_params=_params(),
    )(conf_in, conf_in, conf_in, sc_in, sc_in, sc_in, w31, b31, gln, bln, w3, gc, gs)


def _outproj_kernel(x_ref, ya_ref, yb_ref, w_ref, mod_ref, g_ref, wr_ref, br_ref,
                    xo_ref, h_ref, route_ref, cnt_ref, *, attn_w):
    @pl.when(pl.program_id(0) == 0)
    def _():
        cnt_ref[...] = jnp.zeros(cnt_ref.shape, f32)

    mod = mod_ref[0]
    y = _dot(ya_ref[...], w_ref[0:attn_w, :]) + _dot(yb_ref[...], w_ref[attn_w:, :])
    xn = x_ref[...] + mod[2:3, :] * y
    xo_ref[...] = xn
    h = _modulated(xn, g_ref[...], mod, 3, 4).astype(bf16)
    h_ref[...] = h

    logits = _dot(h, wr_ref[...]) + br_ref[...]
    tm = logits.shape[0]
    lane = lax.broadcasted_iota(jnp.int32, logits.shape, 1).astype(f32)
    vals, idxs = [], []
    chosen = jnp.zeros(logits.shape, f32)
    for _ in range(TOP_K):
        m = jnp.max(logits, axis=-1, keepdims=True)
        first = jnp.min(jnp.where(logits == m, lane, float(LANES)), axis=-1, keepdims=True)
        vals.append(m)
        idxs.append(first)
        hit = lane == first
        chosen = jnp.where(hit, 1.0, chosen)
        logits = jnp.where(hit, NEG_BIG, logits)
    es = [jnp.exp(vv - vals[0]) for vv in vals]
    den = es[0] + es[1] + es[2] + es[3]

    r = lax.broadcasted_iota(jnp.int32, (tm, tm), 0)
    c = lax.broadcasted_iota(jnp.int32, (tm, tm), 1)
    before = _dot(jnp.where(r > c, 1.0, 0.0).astype(bf16), chosen.astype(bf16)) + cnt_ref[0:1, :]
    cnt_ref[...] = cnt_ref[...] + jnp.sum(chosen, axis=0, keepdims=True)

    route = jnp.zeros(logits.shape, f32)
    for k in range(TOP_K):
        rank = jnp.sum(jnp.where(lane == idxs[k], before, 0.0), axis=-1, keepdims=True)
        route = jnp.where(lane == float(ROUTE_IDX + k), idxs[k], route)
        route = jnp.where(lane == float(ROUTE_PROB + k), es[k] / den, route)
        route = jnp.where(lane == float(ROUTE_RANK + k), rank, route)
    route_ref[...] = route


def _outproj_call(x, ya, yb, w, mods, g, wr, br, *, b, s, tile0, n_tiles):
    n, d = x.shape
    tm = TOKEN_TILE
    attn_w = ya.shape[1]
    kern = functools.partial(_outproj_kernel, attn_w=attn_w)
    row = lambda i: (i + tile0, 0)
    local = lambda i: (i, 0)
    fixed = lambda i: (0, 0)
    return pl.pallas_call(
        kern,
        grid=(n_tiles,),
        in_specs=[pl.BlockSpec((tm, d), row),
                  pl.BlockSpec((tm, attn_w), row),
                  pl.BlockSpec((tm, yb.shape[1]), row),
                  pl.BlockSpec(w.shape, fixed),
                  pl.BlockSpec((1, N_ADA, d), lambda i: (_seg_index(i + tile0, tm, s, b), 0, 0)),
                  pl.BlockSpec((1, d), fixed),
                  pl.BlockSpec(wr.shape, fixed),
                  pl.BlockSpec((1, LANES), fixed)],
        out_specs=[pl.BlockSpec((tm, d), row), pl.BlockSpec((tm, d), local),
                   pl.BlockSpec((tm, LANES), local), pl.BlockSpec((8, LANES), fixed)],
        out_shape=[jax.ShapeDtypeStruct((n, d), f32), jax.ShapeDtypeStruct((n_tiles * tm, d), bf16),
                   jax.ShapeDtypeStruct((n_tiles * tm, LANES), f32), jax.ShapeDtypeStruct((8, LANES), f32)],
        input_output_aliases={0: 0},
        compiler_params=_params(),
    )(x, ya, yb, w, mods, g, wr, br)


def _moe_kernel(be_ref, nu_ref, xs_ref, w1_ref, b1_ref, w2_ref, b2_ref, ys_ref):
    @pl.when(pl.program_id(0) < nu_ref[0])
    def _():
        u = _dot(xs_ref[...], w1_ref[0]) + b1_ref[0]
        acts = []
        for grp in range(u.shape[1] // MXU_W):
            gate = jnp.minimum(u[:, grp * MXU_W:grp * MXU_W + LANES], SWIGLU_LIMIT)
            up = jnp.clip(u[:, grp * MXU_W + LANES:(grp + 1) * MXU_W], -SWIGLU_LIMIT, SWIGLU_LIMIT)
            acts.append(((up + 1.0) * (gate * _sigmoid(SWIGLU_ALPHA * gate))).astype(bf16))
        ys_ref[...] = (_dot(jnp.concatenate(acts, axis=1), w2_ref[0]) + b2_ref[0]).astype(bf16)


def _moe_call(blk_expert, n_used, xs, w1, b1, w2, b2):
    cap, d = xs.shape
    bm = EXPERT_TILE
    de2 = w1.shape[2]
    row = lambda i, be, nu: (jnp.minimum(i, nu[0] - 1), 0)
    exp3 = lambda i, be, nu: (be[i], 0, 0)
    return pl.pallas_call(
        _moe_kernel,
        grid_spec=pltpu.PrefetchScalarGridSpec(
            num_scalar_prefetch=2,
            grid=(cap // bm,),
            in_specs=[pl.BlockSpec((bm, d), row),
                      pl.BlockSpec((1, d, de2), exp3),
                      pl.BlockSpec((1, 1, de2), exp3),
                      pl.BlockSpec((1, de2 // 2, d), exp3),
                      pl.BlockSpec((1, 1, d), exp3)],
            out_specs=pl.BlockSpec((bm, d), row)),
        out_shape=jax.ShapeDtypeStruct((cap, d), bf16),
        compiler_params=_params(),
    )(blk_expert, n_used, xs, w1, b1, w2, b2)


def _combine_kernel(x_ref, y0_ref, y1_ref, y2_ref, y3_ref, route_ref, mod_ref, o_ref):
    route = route_ref[...]
    f = jnp.zeros(x_ref.shape, f32)
    for k, y_ref in enumerate((y0_ref, y1_ref, y2_ref, y3_ref)):
        f = f + route[:, ROUTE_PROB + k:ROUTE_PROB + k + 1] * y_ref[...].astype(f32)
    o_ref[...] = x_ref[...] + mod_ref[0][5:6, :] * f


def _combine_call(x, ys4, route, mods, *, b, s, tile0, n_tiles):
    n, d = x.shape
    tm = TOKEN_TILE
    row = lambda i: (i + tile0, 0)
    local = lambda i: (i, 0)
    return pl.pallas_call(
        _combine_kernel,
        grid=(n_tiles,),
        in_specs=[pl.BlockSpec((tm, d), row)] + [pl.BlockSpec((tm, d), local)] * TOP_K +
                 [pl.BlockSpec((tm, LANES), local),
                  pl.BlockSpec((1, N_ADA, d), lambda i: (_seg_index(i + tile0, tm, s, b), 0, 0))],
        out_specs=pl.BlockSpec((tm, d), row),
        out_shape=jax.ShapeDtypeStruct((n, d), f32),
        input_output_aliases={0: 0},
        compiler_params=_params(),
    )(x, *ys4, route, mods)


def _dispatch_plan(route, counts_f, n_experts):
    n_tok = route.shape[0]
    n_pairs = n_tok * TOP_K
    bm = EXPERT_TILE
    pair_bits = (n_pairs - 1).bit_length()
    assert pair_bits + (n_experts - 1).bit_length() < 31
    top_i = route[:, ROUTE_IDX:ROUTE_IDX + TOP_K].astype(jnp.int32)
    rank = route[:, ROUTE_RANK:ROUTE_RANK + TOP_K].astype(jnp.int32)
    counts = counts_f[0, :n_experts].astype(jnp.int32)
    starts = jnp.cumsum(counts) - counts
    padded = (counts + bm - 1) // bm * bm
    pad_ends = jnp.cumsum(padded)
    pad_starts = pad_ends - padded
    experts = jnp.arange(n_experts, dtype=jnp.int32)
    pos = jnp.sum(jnp.where(top_i[:, :, None] == experts, pad_starts, 0), axis=-1) + rank
    n_blocks = -(-n_pairs // bm) + n_experts
    blk_start = jnp.arange(n_blocks, dtype=jnp.int32) * bm
    blk_expert = jnp.minimum(jnp.sum(blk_start[:, None] >= pad_ends[None, :], axis=-1), n_experts - 1).astype(jnp.int32)
    n_used = (pad_ends[-1] // bm).astype(jnp.int32).reshape(1)
    pair_id = jnp.arange(n_pairs, dtype=jnp.int32)
    keys = jnp.sort(top_i.reshape(-1) * (1 << pair_bits) + pair_id)
    sorted_tok = (keys & ((1 << pair_bits) - 1)) // TOP_K
    shift = (starts - pad_starts)[blk_expert]
    src = jnp.clip((blk_start + shift)[:, None] + jnp.arange(bm, dtype=jnp.int32)[None, :], 0, n_pairs - 1)
    tok_buf = sorted_tok[src.reshape(-1)]
    return tok_buf, pos, blk_expert, n_used


def _rope_tables(s, extra_rows):
    rows = s // GRID_W
    axis_dim = HEAD_DIM // 2
    half = axis_dim // 2
    row = jnp.repeat(jnp.arange(rows, dtype=f32), GRID_W)
    col = jnp.tile(jnp.arange(GRID_W, dtype=f32), rows)
    inv = ROPE_THETA ** (-jnp.arange(0, axis_dim, 2, dtype=f32) / axis_dim)
    ang = jnp.concatenate([row[:, None] * inv, col[:, None] * inv], axis=-1)
    lane = jnp.arange(LANES)
    src = ((lane % HEAD_DIM) // axis_dim) * half + lane % half
    sign = jnp.where((lane % axis_dim) < half, -1.0, 1.0).astype(f32)
    cos_t = jnp.cos(ang)[:, src]
    sin_t = jnp.sin(ang)[:, src] * sign
    cos_t = jnp.concatenate([cos_t, jnp.ones((extra_rows, LANES), f32)], axis=0)
    sin_t = jnp.concatenate([sin_t, jnp.zeros((extra_rows, LANES), f32)], axis=0)
    return cos_t, sin_t


def kernel(x, c, ctx, c_ctx, w_ada, b_ada, g_norm_mix, g_norm_ffn, w_in, g_q, g_k, w_conf_dw, b_conf_dw,
           g_conf_ln, b_conf_ln, w_sc_dw, g_grp, w_out, w_router, b_router, w_e_in, b_e_in, w_e_out, b_e_out):
    b, s, d = x.shape
    n_ctx = ctx.shape[1]
    depth = w_ada.shape[0]
    n_experts = w_router.shape[2]
    d_exp = w_e_out.shape[2]
    attn_w = d // 2
    n_q = attn_w // HEAD_DIM
    n_kv = max(n_q // 4, 1)
    kv_w = n_kv * HEAD_DIM
    conf_w = d // 4
    sconv_w = d - attn_w - conf_w
    assert n_kv == 2 and kv_w == LANES and conf_w == sconv_w
    n_lat = b * s
    n_tiles = (n_lat + b * n_ctx) // TOKEN_TILE
    part_sizes = [n_tiles // MOE_PARTS + (1 if p < n_tiles % MOE_PARTS else 0) for p in range(MOE_PARTS)]
    part_ranges = [(sum(part_sizes[:p]), part_sizes[p]) for p in range(MOE_PARTS)]
    assert s % TOKEN_TILE == 0 and (b * n_ctx) % TOKEN_TILE == 0 and s % KEY_CHUNK == 0 and s % Q_TILE == 0

    w_in_b = w_in.astype(bf16)
    w_out_b = w_out.astype(bf16)
    gq_t = jnp.tile(g_q, (1, n_q)).reshape(depth, 1, attn_w)
    gk_t = jnp.tile(g_k, (1, n_kv)).reshape(depth, 1, kv_w)
    cos_t, sin_t = _rope_tables(s, TOKEN_TILE)

    w1 = _weights_call(_w1_prep_kernel, w_e_in.reshape(depth * n_experts, d, 2 * d_exp), WEIGHT_ROWS)
    w2 = _weights_call(_cast_kernel, w_e_out.reshape(depth * n_experts, d_exp, d), WEIGHT_ROWS)
    b1 = b_e_in.reshape(depth * n_experts, 2 * d_exp // MXU_W, LANES, 2)
    b1 = jnp.swapaxes(b1, -1, -2).reshape(depth * n_experts, 1, 2 * d_exp)
    b2 = b_e_out.reshape(depth * n_experts, 1, d)
    wr = jnp.pad(w_router, ((0, 0), (0, 0), (0, LANES - n_experts))).astype(bf16)
    br = jnp.pad(b_router, ((0, 0), (0, LANES - n_experts)), constant_values=NEG_BIG).reshape(depth, 1, LANES)

    mod_rows = b + 1
    pad_rows = -mod_rows % 8
    cc = jnp.concatenate([c, c_ctx[None, :], jnp.zeros((pad_rows, d), f32)], axis=0)
    mods = _ada_call(cc, w_ada, b_ada)[:, :mod_rows].reshape(depth, mod_rows, N_ADA, d)

    xf = jnp.concatenate([x.reshape(n_lat, d), ctx.reshape(b * n_ctx, d)], axis=0)
    for l in range(depth):
        qt, k, vt, conf_in, sc_in = _inproj_call(
            xf, mods[l], g_norm_mix[l][None], w_in_b[l], gq_t[l], gk_t[l], cos_t, sin_t,
            b=b, s=s, attn_w=attn_w, kv_w=kv_w, conf_w2=2 * conf_w, sc_w3=3 * sconv_w)
        g_attn = g_grp[l][None, :attn_w]
        ya = _attn_call(qt, k, vt, g_attn, None, b=b, s=s, ctx=n_ctx, latent=True)
        ya = _attn_call(qt, k, vt, g_attn, ya, b=b, s=s, ctx=n_ctx, latent=False)
        yb = _conv_call(conf_in, sc_in, w_conf_dw[l], b_conf_dw[l][None], g_conf_ln[l][None], b_conf_ln[l][None],
                        w_sc_dw[l], g_grp[l][None, attn_w:attn_w + conf_w], g_grp[l][None, attn_w + conf_w:],
                        b=b, s=s, ctx=n_ctx)
        parts = []
        for tile0, part_tiles in part_ranges:
            xf, h2, route, counts = _outproj_call(xf, ya, yb, w_out_b[l], mods[l], g_norm_ffn[l][None], wr[l], br[l],
                                                  b=b, s=s, tile0=tile0, n_tiles=part_tiles)
            parts.append((h2, route, counts))
        gathered = []
        for h2, route, counts in parts:
            tok_buf, pos, blk_expert, n_used = _dispatch_plan(route, counts, n_experts)
            ys = _moe_call(blk_expert + l * n_experts, n_used, h2[tok_buf], w1, b1, w2, b2)
            gathered.append([ys[pos[:, kk]] for kk in range(TOP_K)])
        for (tile0, part_tiles), ys4, (h2, route, counts) in zip(part_ranges, gathered, parts):
            xf = _combine_call(xf, ys4, route, mods[l], b=b, s=s, tile0=tile0, n_tiles=part_tiles)
    return xf[:n_lat].reshape(b, s, d)
```
